```python
import math
import jax
import jax.numpy as jnp
from jax import lax
import numpy as np

D_MODEL = 1024
BATCH = 8
SEQ = 4096
DEPTH = 2

EPS = 1e-6
POOL_WINDOWS = (2, 4, 8, 16)
POOL_GROUP = D_MODEL // 16
POOL_WIDTH = POOL_GROUP * len(POOL_WINDOWS)
M_HEADS = 4
M_HEAD_DIM = D_MODEL // 16
M_WIDTH = M_HEADS * M_HEAD_DIM
M_CONV = 4
M_CHUNK = 64
D_HEADS = 4
D_HEAD_DIM = D_MODEL // 16
D_V_DIM = 2 * D_HEAD_DIM
D_QK_WIDTH = D_HEADS * 2 * D_HEAD_DIM
D_WIDTH = D_HEADS * D_V_DIM
Q_BLOCK = 128
ROPE_THETA = 10000.0
N_BRANCH = 3
D_FF = ((8 * D_MODEL // 3 + 127) // 128) * 128
IN_SPLITS = (POOL_WIDTH, M_WIDTH, M_WIDTH, M_WIDTH, M_WIDTH, 2 * M_HEADS, D_QK_WIDTH, D_QK_WIDTH, D_WIDTH, N_BRANCH * D_MODEL)
N_IN = sum(IN_SPLITS)

kernel_name = "hybrid_pool_mlstm_diffattn_macaron"


def rmsnorm(x, g):
    xf = x.astype(jnp.float32)
    y = xf * lax.rsqrt(jnp.mean(xf * xf, axis=-1, keepdims=True) + EPS)
    return (y * g).astype(x.dtype)


def swiglu(x, w13, w2):
    g, u = jnp.split(x @ w13, 2, axis=-1)
    return (jax.nn.silu(g) * u) @ w2


def rope_tables(positions, dim):
    inv = 1.0 / (ROPE_THETA ** (jnp.arange(0, dim, 2, dtype=jnp.float32) / dim))
    ang = positions.astype(jnp.float32)[..., None] * inv
    return jnp.cos(ang)[:, :, None, :], jnp.sin(ang)[:, :, None, :]


def apply_rope(x, cos, sin):
    x1, x2 = jnp.split(x.astype(jnp.float32), 2, axis=-1)
    return jnp.concatenate([x1 * cos - x2 * sin, x2 * cos + x1 * sin], axis=-1).astype(x.dtype)


def pool_mixer(u, w_grp, scale):
    B, S, _ = u.shape
    uf = u.astype(jnp.float32)
    cs = jnp.concatenate([jnp.zeros((B, 1, POOL_WIDTH), jnp.float32), jnp.cumsum(uf, axis=1)], axis=1)
    t = jnp.arange(S)
    outs = []
    for g, w in enumerate(POOL_WINDOWS):
        c = cs[:, :, g * POOL_GROUP:(g + 1) * POOL_GROUP]
        lo = jnp.maximum(t + 1 - w, 0)
        win_sum = c[:, 1:] - c[:, lo]
        cnt = jnp.minimum(t + 1, w).astype(jnp.float32)[None, :, None]
        outs.append(win_sum / cnt - uf[:, :, g * POOL_GROUP:(g + 1) * POOL_GROUP])
    pooled = jnp.stack(outs, axis=2).astype(u.dtype)
    mixed = jnp.einsum('bsgc,gcd->bsgd', pooled, w_grp)
    return mixed.reshape(B, S, POOL_WIDTH) * scale


def causal_depthwise_conv(x, w, b):
    C = x.shape[-1]
    y = lax.conv_general_dilated(x, w[:, None, :], window_strides=(1,), padding=((M_CONV - 1, 0),),
                                 dimension_numbers=('NWC', 'WIO', 'NWC'), feature_group_count=C)
    return y + b


def mlstm_chunkwise(q, k, v, i_pre, f_pre):
    B, H, S, dh = q.shape
    L = M_CHUNK
    NC = S // L
    qf = q.astype(jnp.float32)
    kf = k.astype(jnp.float32) / math.sqrt(dh)
    vf = v.astype(jnp.float32)
    logf = jax.nn.log_sigmoid(f_pre)

    def chunks(a):
        return jnp.moveaxis(a.reshape((B, H, NC, L) + a.shape[3:]), 2, 0)

    causal = jnp.tril(jnp.ones((L, L), dtype=bool))

    def step(carry, xs):
        Cm, nv, m = carry
        qc, kc, vc, ic, lfc = xs
        bcum = jnp.cumsum(lfc, axis=-1)
        dmat = bcum[..., :, None] - bcum[..., None, :] + ic[..., None, :]
        dmat = jnp.where(causal, dmat, -jnp.inf)
        inter = bcum + m[..., None]
        m_t = jnp.maximum(inter, jnp.max(dmat, axis=-1))
        w_intra = jnp.exp(dmat - m_t[..., None])
        w_inter = jnp.exp(inter - m_t)
        sc = jnp.einsum('bhtd,bhsd->bhts', qc, kc) * w_intra
        num = jnp.einsum('bhts,bhsd->bhtd', sc, vc) + w_inter[..., None] * jnp.einsum('bhed,bhtd->bhte', Cm, qc)
        den = jnp.sum(sc, axis=-1) + w_inter * jnp.einsum('bhd,bhtd->bht', nv, qc)
        h = num / jnp.maximum(jnp.abs(den), jnp.exp(-m_t))[..., None]
        b_last = bcum[..., -1]
        g = b_last[..., None] - bcum + ic
        m_new = jnp.maximum(b_last + m, jnp.max(g, axis=-1))
        decay = jnp.exp(b_last + m - m_new)
        wk = jnp.exp(g - m_new[..., None])
        C_new = decay[..., None, None] * Cm + jnp.einsum('bhs,bhse,bhsd->bhed', wk, vc, kc)
        n_new = decay[..., None] * nv + jnp.einsum('bhs,bhsd->bhd', wk, kc)
        return (C_new, n_new, m_new), h

    init = (jnp.zeros((B, H, dh, dh), jnp.float32), jnp.zeros((B, H, dh), jnp.float32), jnp.zeros((B, H), jnp.float32))
    _, hs = lax.scan(step, init, (chunks(qf), chunks(kf), chunks(vf), chunks(i_pre), chunks(logf)))
    return jnp.moveaxis(hs, 0, 2).reshape(B, H, S, dh)


def mlstm_branch(m_q, m_k, m_v, m_o, m_if, conv_w, conv_b, gate_b, norm_g):
    B, S, _ = m_q.shape
    qk = jax.nn.silu(causal_depthwise_conv(jnp.concatenate([m_q, m_k], axis=-1), conv_w, conv_b))
    q, k = jnp.split(qk, 2, axis=-1)

    def heads(a):
        return a.reshape(B, S, M_HEADS, M_HEAD_DIM).transpose(0, 2, 1, 3)

    gates = (m_if + gate_b).astype(jnp.float32).transpose(0, 2, 1)
    i_pre, f_pre = gates[:, :M_HEADS], gates[:, M_HEADS:]
    h = mlstm_chunkwise(heads(q), heads(k), heads(m_v), i_pre, f_pre)
    mu = jnp.mean(h, axis=-1, keepdims=True)
    var = jnp.mean(jnp.square(h - mu), axis=-1, keepdims=True)
    h = (h - mu) * lax.rsqrt(var + EPS)
    h = h.transpose(0, 2, 1, 3).reshape(B, S, M_WIDTH) * norm_g
    return (jax.nn.sigmoid(m_o.astype(jnp.float32)) * h).astype(m_q.dtype)


def diff_attention(q, k, v, lam, norm_g, lambda_init):
    B, S = q.shape[0], q.shape[1]
    qh = q.transpose(0, 2, 1, 3) * (D_HEAD_DIM ** -0.5)
    kh = k.transpose(0, 2, 1, 3)
    vh = v.transpose(0, 2, 1, 3)
    lamf = lam.astype(jnp.float32)
    lam_full = jnp.exp(jnp.sum(lamf[0] * lamf[1])) - jnp.exp(jnp.sum(lamf[2] * lamf[3])) + lambda_init
    NB = S // Q_BLOCK
    qb = qh.reshape(B, 2 * D_HEADS, NB, Q_BLOCK, D_HEAD_DIM).transpose(2, 0, 1, 3, 4)
    kpos = jnp.arange(S)

    def block(args):
        qblk, start = args
        s = jnp.einsum('bhqd,bhkd->bhqk', qblk, kh).astype(jnp.float32)
        qpos = start + jnp.arange(Q_BLOCK)
        s = jnp.where(kpos[None, :] <= qpos[:, None], s, -jnp.inf)
        p = jax.nn.softmax(s, axis=-1).reshape(B, D_HEADS, 2, Q_BLOCK, S)
        a = p[:, :, 0] - lam_full * p[:, :, 1]
        return jnp.einsum('bhqk,bhkd->bhqd', a.astype(vh.dtype), vh)

    out = lax.map(block, (qb, jnp.arange(NB, dtype=jnp.int32) * Q_BLOCK))
    out = out.transpose(1, 2, 0, 3, 4).reshape(B, D_HEADS, S, D_V_DIM).astype(jnp.float32)
    out = out * lax.rsqrt(jnp.mean(out * out, axis=-1, keepdims=True) + EPS) * norm_g
    out = out * (1.0 - lambda_init)
    return out.transpose(0, 2, 1, 3).reshape(B, S, D_WIDTH).astype(v.dtype)


def setup_inputs(seed: int = 0) -> dict:
    key = jax.random.key(seed)
    ks = jax.random.split(key, 24)
    f32 = jnp.float32
    L = DEPTH

    def nrm(k, shape, scale):
        return jax.random.normal(k, shape, f32) * scale

    def gain(k, shape):
        return 1.0 + 0.02 * jax.random.normal(k, shape, f32)

    f_bias = jnp.broadcast_to(jnp.linspace(3.0, 6.0, M_HEADS, dtype=f32), (L, M_HEADS))
    m_gate_b = jnp.concatenate([nrm(ks[10], (L, M_HEADS), 0.1), f_bias + nrm(ks[11], (L, M_HEADS), 0.1)], axis=-1)
    return {
        "x": jax.random.normal(ks[0], (BATCH, SEQ, D_MODEL), f32),
        "positions": jnp.broadcast_to(jnp.arange(SEQ, dtype=jnp.int32), (BATCH, SEQ)),
        "ffn1_norm": gain(ks[1], (L, D_MODEL)),
        "ffn1_w13": nrm(ks[2], (L, D_MODEL, 2 * D_FF), D_MODEL ** -0.5),
        "ffn1_w2": nrm(ks[3], (L, D_FF, D_MODEL), D_FF ** -0.5),
        "mix_norm": gain(ks[4], (L, D_MODEL)),
        "w_in": nrm(ks[5], (L, D_MODEL, N_IN), D_MODEL ** -0.5),
        "pool_w": nrm(ks[6], (L, len(POOL_WINDOWS), POOL_GROUP, POOL_GROUP), POOL_GROUP ** -0.5),
        "pool_scale": gain(ks[7], (L, POOL_WIDTH)),
        "m_conv_w": nrm(ks[8], (L, M_CONV, 2 * M_WIDTH), M_CONV ** -0.5),
        "m_conv_b": nrm(ks[9], (L, 2 * M_WIDTH), 0.01),
        "m_gate_b": m_gate_b,
        "m_norm": gain(ks[12], (L, M_WIDTH)),
        "d_lambda": nrm(ks[13], (L, 4, D_HEAD_DIM), 0.1),
        "d_norm": gain(ks[14], (L, D_V_DIM)),
        "p_a": nrm(ks[15], (L, POOL_WIDTH, D_MODEL), POOL_WIDTH ** -0.5),
        "p_b": nrm(ks[16], (L, M_WIDTH, D_MODEL), M_WIDTH ** -0.5),
        "p_c": nrm(ks[17], (L, D_WIDTH, D_MODEL), D_WIDTH ** -0.5),
        "w_out": nrm(ks[18], (L, D_MODEL, D_MODEL), D_MODEL ** -0.5),
        "ffn2_norm": gain(ks[19], (L, D_MODEL)),
        "ffn2_w13": nrm(ks[20], (L, D_MODEL, 2 * D_FF), D_MODEL ** -0.5),
        "ffn2_w2": nrm(ks[21], (L, D_FF, D_MODEL), D_FF ** -0.5),
        "final_norm": gain(ks[22], (D_MODEL,)),
    }


def reference(x, positions, ffn1_norm, ffn1_w13, ffn1_w2, mix_norm, w_in, pool_w, pool_scale, m_conv_w, m_conv_b, m_gate_b, m_norm, d_lambda, d_norm, p_a, p_b, p_c, w_out, ffn2_norm, ffn2_w13, ffn2_w2, final_norm):
    B, S, _ = x.shape
    cos, sin = rope_tables(positions, D_HEAD_DIM)
    split_at = np.cumsum(IN_SPLITS)[:-1].tolist()
    for l in range(DEPTH):
        lambda_init = 0.8 - 0.6 * math.exp(-0.3 * l)
        x = x + 0.5 * swiglu(rmsnorm(x, ffn1_norm[l]), ffn1_w13[l], ffn1_w2[l])
        h = rmsnorm(x, mix_norm[l])
        z = jnp.einsum('bsd,dn->bsn', h, w_in[l])
        (u_pool, m_q, m_k, m_v, m_o, m_if, d_q, d_k, d_v, gate_pre) = jnp.split(z, split_at, axis=-1)
        y_a = pool_mixer(u_pool, pool_w[l], pool_scale[l])
        y_b = mlstm_branch(m_q, m_k, m_v, m_o, m_if, m_conv_w[l], m_conv_b[l], m_gate_b[l], m_norm[l])
        dq = apply_rope(d_q.reshape(B, S, 2 * D_HEADS, D_HEAD_DIM), cos, sin)
        dk = apply_rope(d_k.reshape(B, S, 2 * D_HEADS, D_HEAD_DIM), cos, sin)
        y_c = diff_attention(dq, dk, d_v.reshape(B, S, D_HEADS, D_V_DIM), d_lambda[l], d_norm[l], lambda_init)
        g_a, g_b, g_c = jnp.split(jax.nn.sigmoid(gate_pre), N_BRANCH, axis=-1)
        merged = g_a * (y_a @ p_a[l]) + g_b * (y_b @ p_b[l]) + g_c * (y_c @ p_c[l])
        x = x + merged @ w_out[l]
        x = x + 0.5 * swiglu(rmsnorm(x, ffn2_norm[l]), ffn2_w13[l], ffn2_w2[l])
    return rmsnorm(x, final_norm)
```

```python
import functools
import math

import jax
import jax.numpy as jnp
import numpy as np
from jax import lax
from jax.experimental import pallas as pl
from jax.experimental.pallas import tpu as pltpu

F32 = jnp.float32
BF16 = jnp.bfloat16
EPS = 1e-6
ROPE_THETA = 10000.0
POOL_WINDOWS = (2, 4, 8, 16)
N_HEADS = 4
HEAD_DIM = 64
LANES = 128
MLSTM_CHUNK = 128
POOL_HALO = 16
VMEM_LIMIT = 56 * 1024 * 1024


def _params(*sem):
    return pltpu.CompilerParams(dimension_semantics=sem, vmem_limit_bytes=VMEM_LIMIT)


def _resident(shape):
    zeros = (0,) * len(shape)
    return pl.BlockSpec(shape, lambda *_: zeros, pipeline_mode=pl.Buffered(1))


def _rms(x, g):
    return x * lax.rsqrt(jnp.mean(x * x, axis=-1, keepdims=True) + EPS) * g


def _sigmoid(x):
    return 1.0 / (1.0 + jnp.exp(-x))


def _dot(a, b):
    return jnp.dot(a, b, preferred_element_type=F32)


def _dot_nt(a, b):
    return lax.dot_general(a, b, (((1,), (1,)), ((), ())), preferred_element_type=F32)


def _dot_tn(a, b):
    return lax.dot_general(a, b, (((0,), (0,)), ((), ())), preferred_element_type=F32)


def _rope_kernel(pos_ref, inv_ref, cos_ref, sin_ref, nsin_ref):
    ang = pos_ref[...].astype(F32) * inv_ref[...]
    s = jnp.sin(ang)
    cos_ref[...] = jnp.cos(ang)
    sin_ref[...] = s
    nsin_ref[...] = -s


def _rope_tables(positions):
    t = positions.size
    half = HEAD_DIM // 2
    per_row = LANES // half
    rows = t // per_row
    pos = jnp.repeat(positions.reshape(rows, per_row), half, axis=1)
    inv = 1.0 / (ROPE_THETA ** (jnp.arange(0, HEAD_DIM, 2, dtype=F32) / HEAD_DIM))
    inv = jnp.tile(inv, per_row).reshape(1, LANES)
    tr = min(rows, 1024)
    spec = pl.BlockSpec((tr, LANES), lambda i: (i, 0))
    outs = pl.pallas_call(
        _rope_kernel,
        out_shape=[jax.ShapeDtypeStruct((rows, LANES), F32)] * 3,
        grid=(rows // tr,),
        in_specs=[spec, pl.BlockSpec((1, LANES), lambda i: (0, 0))],
        out_specs=[spec] * 3,
        compiler_params=_params("arbitrary"),
        name="rope_tables",
    )(pos, inv)
    cos, sin, nsin = (o.reshape(t, half) for o in outs)
    return jnp.concatenate([cos] * 4, axis=1), jnp.concatenate([nsin, nsin, sin, sin], axis=1)


def _ffn_kernel(x_ref, g_ref, w13_ref, w2_ref, *rest, d_ff, tf, final):
    if final:
        fg_ref, o_ref, acc_ref = rest
    else:
        o_ref, acc_ref = rest
    x = x_ref[...]
    xn = _rms(x, g_ref[...]).astype(BF16)
    for c in range(d_ff // tf):
        lo, hi = c * tf, (c + 1) * tf
        gate = _dot(xn, w13_ref[:, lo:hi])
        up = _dot(xn, w13_ref[:, d_ff + lo:d_ff + hi])
        act = (gate * _sigmoid(gate) * up).astype(BF16)
        part = _dot(act, w2_ref[lo:hi, :])
        if c == 0:
            acc_ref[...] = part
        else:
            acc_ref[...] += part
    y = x + 0.5 * acc_ref[...]
    if final:
        y = _rms(y, fg_ref[...])
    o_ref[...] = y


def _ffn(x, g, w13, w2, final_g=None, *, tm=512, tf=256):
    t, d = x.shape
    d_ff = w2.shape[0]
    tm = min(tm, t)
    final = final_g is not None
    row = pl.BlockSpec((tm, d), lambda i: (i, 0))
    in_specs = [row, _resident((1, d)), _resident((d, 2 * d_ff)), _resident((d_ff, d))]
    args = [x, g.reshape(1, d), w13, w2]
    if final:
        in_specs.append(_resident((1, d)))
        args.append(final_g.reshape(1, d))
    return pl.pallas_call(
        functools.partial(_ffn_kernel, d_ff=d_ff, tf=tf, final=final),
        out_shape=jax.ShapeDtypeStruct((t, d), F32),
        grid=(t // tm,),
        in_specs=in_specs,
        out_specs=row,
        scratch_shapes=[pltpu.VMEM((tm, d), F32)],
        compiler_params=_params("arbitrary"),
        name="ffn_final" if final else "ffn",
    )(*args)


_C_MQK, _C_MV, _C_MO, _C_POOL, _C_DQ, _C_DK, _C_DV, _C_GATE, _C_END = 0, 512, 768, 1024, 1280, 1792, 2304, 2816, 5888


def _inproj_kernel(x_ref, g_ref, w_ref, wift_ref, cos_ref, sin_ref,
                   mqk_ref, mv_ref, mo_ref, pool_ref, ift_ref, dq_ref, dk_ref, dv_ref, gate_ref):
    xn = _rms(x_ref[...], g_ref[...]).astype(BF16)

    def proj(lo, hi):
        return _dot(xn, w_ref[:, lo:hi])

    mqk_ref[...] = proj(_C_MQK, _C_MV)
    mv_ref[...] = proj(_C_MV, _C_MO)
    mo_ref[...] = proj(_C_MO, _C_POOL)
    pool_ref[...] = proj(_C_POOL, _C_DQ)
    ift_ref[...] = _dot_nt(wift_ref[...], xn)
    cos = cos_ref[...]
    sin = sin_ref[...]
    for ref, lo, scale in ((dq_ref, _C_DQ, HEAD_DIM ** -0.5), (dk_ref, _C_DK, 1.0)):
        z = proj(lo, lo + N_HEADS * LANES)
        for h in range(N_HEADS):
            zh = z[:, h * LANES:(h + 1) * LANES]
            r = zh * cos + pltpu.roll(zh, LANES // 2, 1) * sin
            ref[:, h * LANES:(h + 1) * LANES] = (r * scale).astype(BF16)
    dv_ref[...] = proj(_C_DV, _C_GATE).astype(BF16)
    gate_ref[...] = proj(_C_GATE, _C_END).astype(BF16)


def _inproj(x, g, w_big, w_ift, cosf, sinf, *, tm=512):
    t, d = x.shape
    tm = min(tm, t)
    widths = (512, 256, 256, 256)

    def rows(width):
        return pl.BlockSpec((tm, width), lambda i: (i, 0))

    out_shape = [jax.ShapeDtypeStruct((t, w), F32) for w in widths]
    out_specs = [rows(w) for w in widths]
    out_shape.append(jax.ShapeDtypeStruct((8, t), F32))
    out_specs.append(pl.BlockSpec((8, tm), lambda i: (0, i)))
    for w in (512, 512, 512, 3072):
        out_shape.append(jax.ShapeDtypeStruct((t, w), BF16))
        out_specs.append(rows(w))
    return pl.pallas_call(
        _inproj_kernel,
        out_shape=out_shape,
        grid=(t // tm,),
        in_specs=[rows(d), _resident((1, d)), _resident(w_big.shape), _resident(w_ift.shape),
                  rows(LANES), rows(LANES)],
        out_specs=out_specs,
        compiler_params=_params("arbitrary"),
        name="inproj",
    )(x, g.reshape(1, d), w_big, w_ift, cosf, sinf)


def _head_of_lane(shape, dim):
    return lax.broadcasted_iota(jnp.int32, shape, dim) // HEAD_DIM


def _per_head_to_lanes(cols, lane_head):
    out = cols[N_HEADS - 1]
    for h in range(N_HEADS - 2, -1, -1):
        out = jnp.where(lane_head == h, cols[h], out)
    return out


def _mlstm_kernel(qk_ref, v_ref, o_ref, ift_ref, cw_ref, cb_ref, gb_ref, ng_ref, y_ref,
                  ext_ref, ct_ref, n_ref, m_ref, *, tb):
    L = MLSTM_CHUNK
    W = N_HEADS * HEAD_DIM
    kconv = cw_ref.shape[0]

    @pl.when(pl.program_id(1) == 0)
    def _():
        ext_ref[0:8, :] = jnp.zeros((8, 2 * W), F32)
        ct_ref[...] = jnp.zeros_like(ct_ref)
        n_ref[...] = jnp.zeros_like(n_ref)
        m_ref[...] = jnp.zeros_like(m_ref)

    ext_ref[8:8 + tb, :] = qk_ref[...]
    conv = cb_ref[...] + cw_ref[kconv - 1:kconv, :] * ext_ref[8:8 + tb, :]
    for j in range(kconv - 1):
        off = 8 - (kconv - 1) + j
        conv = conv + cw_ref[j:j + 1, :] * ext_ref[off:off + tb, :]
    ext_ref[0:8, :] = ext_ref[tb:tb + 8, :]
    qk = conv * _sigmoid(conv)
    q_all = qk[:, :W]
    k_all = qk[:, W:] * (HEAD_DIM ** -0.5)

    lane_head = _head_of_lane((L, W), 1)
    lane_head_row = _head_of_lane((1, W), 1)
    tt = lax.broadcasted_iota(jnp.int32, (L, L), 0)
    ss = lax.broadcasted_iota(jnp.int32, (L, L), 1)
    causal = ss <= tt
    diag = ss == tt
    lane8 = lax.broadcasted_iota(jnp.int32, (8, L), 1)
    bd_mask = _head_of_lane((W, W), 0) == _head_of_lane((W, W), 1)

    ct = ct_ref[...]
    nrow = n_ref[...]
    m = [m_ref[h:h + 1, 0:1] for h in range(N_HEADS)]

    for c in range(tb // L):
        r0 = c * L
        q = q_all[r0:r0 + L, :]
        k = k_all[r0:r0 + L, :]
        v = v_ref[r0:r0 + L, :]
        vb = v.astype(BF16)
        gates = ift_ref[:, r0:r0 + L] + gb_ref[...]
        logf = jnp.minimum(gates, 0.0) - jnp.log1p(jnp.exp(-jnp.abs(gates)))
        bcum = logf
        for sh in (1, 2, 4, 8, 16, 32, 64):
            bcum = bcum + jnp.where(lane8 >= sh, pltpu.roll(bcum, sh, 1), 0.0)

        qm = [jnp.where(lane_head == h, q, 0.0) for h in range(N_HEADS)]
        s_all = _dot_nt(jnp.concatenate(qm, axis=0).astype(BF16), k.astype(BF16))
        bcol, rmax, den0, wk0, blast, gmax, sc0 = [], [], [], [], [], [], []
        for h in range(N_HEADS):
            i_row = gates[h:h + 1, :]
            lf_row = logf[N_HEADS + h:N_HEADS + h + 1, :]
            b_row = bcum[N_HEADS + h:N_HEADS + h + 1, :]
            b_col = jnp.sum(jnp.where(causal, lf_row, 0.0), axis=-1, keepdims=True)
            i_col = jnp.sum(jnp.where(diag, i_row, 0.0), axis=-1, keepdims=True)
            dm = jnp.where(causal, b_col + (i_row - b_row), -jnp.inf)
            r_max = jnp.max(dm, axis=-1, keepdims=True)
            sc = s_all[h * L:(h + 1) * L, :] * jnp.exp(dm - r_max)
            b_last = jnp.sum(lf_row, axis=-1, keepdims=True)
            g_row = b_last - b_row + i_row
            g_max = jnp.max(g_row, axis=-1, keepdims=True)
            bcol.append(b_col)
            rmax.append(r_max)
            den0.append(jnp.sum(sc, axis=-1, keepdims=True))
            wk0.append(jnp.exp(b_last - b_col + i_col - g_max))
            blast.append(b_last)
            gmax.append(g_max)
            sc0.append(sc)
        r_all = _dot(jnp.concatenate(sc0, axis=0).astype(BF16), vb)
        kw = k * _per_head_to_lanes(wk0, lane_head)
        dct = jnp.where(bd_mask, _dot_tn(kw.astype(BF16), vb), 0.0)
        dn = jnp.sum(kw, axis=0, keepdims=True)

        qc = _dot(q.astype(BF16), ct.astype(BF16))
        qn = q * nrow
        z, decay, grow = [], [], []
        for h in range(N_HEADS):
            inter = bcol[h] + m[h]
            m_t = jnp.maximum(inter, rmax[h])
            a = jnp.exp(rmax[h] - m_t)
            b = jnp.exp(inter - m_t)
            qn_h = jnp.sum(jnp.where(lane_head == h, qn, 0.0), axis=-1, keepdims=True)
            den = a * den0[h] + b * qn_h
            inv = 1.0 / jnp.maximum(jnp.abs(den), jnp.exp(-m_t))
            z.append((a * inv) * r_all[h * L:(h + 1) * L, :] + (b * inv) * qc)
            m_new = jnp.maximum(blast[h] + m[h], gmax[h])
            decay.append(jnp.exp(blast[h] + m[h] - m_new))
            grow.append(jnp.exp(gmax[h] - m_new))
            m[h] = m_new
        hout = _per_head_to_lanes(z, lane_head)
        decay_row = _per_head_to_lanes(decay, lane_head_row)
        grow_row = _per_head_to_lanes(grow, lane_head_row)
        ct = decay_row * ct + grow_row * dct
        nrow = decay_row * nrow + grow_row * dn

        mu = [jnp.sum(jnp.where(lane_head == h, hout, 0.0), axis=-1, keepdims=True) for h in range(N_HEADS)]
        dlt = hout - _per_head_to_lanes(mu, lane_head) * (1.0 / HEAD_DIM)
        var = [jnp.sum(jnp.where(lane_head == h, dlt * dlt, 0.0), axis=-1, keepdims=True) for h in range(N_HEADS)]
        rstd = lax.rsqrt(_per_head_to_lanes(var, lane_head) * (1.0 / HEAD_DIM) + EPS)
        y = _sigmoid(o_ref[r0:r0 + L, :]) * (dlt * rstd * ng_ref[...])
        y_ref[r0:r0 + L, :] = y.astype(y_ref.dtype)

    ct_ref[...] = ct
    n_ref[...] = nrow
    for h in range(N_HEADS):
        m_ref[h:h + 1, :] = jnp.broadcast_to(m[h], (1, LANES))


def _mlstm(mqk, mv, mo, ift, conv_w, conv_b, gate_b, norm_g, *, batch, tb=512):
    t, w2 = mqk.shape
    w = w2 // 2
    s = t // batch
    tb = min(tb, s)
    nb = s // tb

    def rows(width):
        return pl.BlockSpec((tb, width), lambda b, i: (b * nb + i, 0))

    return pl.pallas_call(
        functools.partial(_mlstm_kernel, tb=tb),
        out_shape=jax.ShapeDtypeStruct((t, w), BF16),
        grid=(batch, nb),
        in_specs=[rows(w2), rows(w), rows(w), pl.BlockSpec((8, tb), lambda b, i: (0, b * nb + i)),
                  _resident(conv_w.shape), _resident((1, w2)), _resident((8, LANES)), _resident((1, w))],
        out_specs=rows(w),
        scratch_shapes=[pltpu.VMEM((tb + 8, w2), F32), pltpu.VMEM((w, w), F32),
                        pltpu.VMEM((1, w), F32), pltpu.VMEM((8, LANES), F32)],
        compiler_params=_params("arbitrary", "arbitrary"),
        name="mlstm",
    )(mqk, mv, mo, ift, conv_w, conv_b.reshape(1, w2),
      jnp.broadcast_to(gate_b.reshape(8, 1), (8, LANES)), norm_g.reshape(1, w))


def _attn_kernel(q_ref, k_ref, v_ref, lam_ref, ng_ref, o_ref, qs_ref, m_ref, l_ref, acc_ref, *, tq, lambda_init):
    i = pl.program_id(2)
    q = q_ref[...]
    comp_a = (lax.broadcasted_iota(jnp.int32, q.shape, 1) % HEAD_DIM) < HEAD_DIM // 2
    zero = jnp.zeros_like(q)
    qs_ref[0:tq, :] = jnp.where(comp_a, q, zero)
    qs_ref[tq:2 * tq, :] = jnp.where(comp_a, zero, q)
    m_ref[...] = jnp.full_like(m_ref, -jnp.inf)
    l_ref[...] = jnp.zeros_like(l_ref)
    acc_ref[...] = jnp.zeros_like(acc_ref)
    rep = tq // LANES

    def step(j, masked):
        start = pl.multiple_of(j * tq, tq)
        k = k_ref[pl.ds(start, tq), :]
        v = v_ref[pl.ds(start, tq), :]
        s = _dot_nt(qs_ref[...], k)
        if masked:
            row = lax.broadcasted_iota(jnp.int32, s.shape, 0) % tq
            col = lax.broadcasted_iota(jnp.int32, s.shape, 1)
            s = jnp.where(col <= row, s, -jnp.inf)
        m_prev = m_ref[...]
        m_next = jnp.maximum(m_prev, jnp.max(s, axis=-1, keepdims=True))
        alpha = jnp.exp(m_prev - m_next)
        p = jnp.exp(s - pltpu.repeat(m_next, rep, 1))
        l_ref[...] = alpha * l_ref[...] + jnp.sum(p, axis=-1, keepdims=True)
        acc_ref[...] = alpha * acc_ref[...] + _dot(p.astype(BF16), v)
        m_ref[...] = m_next

    def body(j, carry):
        step(j, False)
        return carry

    lax.fori_loop(0, i, body, 0)
    step(i, True)

    o = acc_ref[...] * (1.0 / l_ref[...])
    lam = lam_ref[...]
    lam_full = (jnp.exp(jnp.sum(lam[0:1, :] * lam[1:2, :], axis=-1, keepdims=True))
                - jnp.exp(jnp.sum(lam[2:3, :] * lam[3:4, :], axis=-1, keepdims=True)) + lambda_init)
    out = o[0:tq, :] - lam_full * o[tq:2 * tq, :]
    out = out * lax.rsqrt(jnp.mean(out * out, axis=-1, keepdims=True) + EPS) * ng_ref[...] * (1.0 - lambda_init)
    o_ref[...] = out.astype(o_ref.dtype)


def _attn(dq, dk, dv, lam, norm_g, lambda_init, *, batch, tq=512):
    t, w = dq.shape
    s = t // batch
    tq = min(tq, s)
    nq = s // tq
    q_spec = pl.BlockSpec((tq, LANES), lambda b, h, i: (b * nq + i, h))
    kv_spec = pl.BlockSpec((s, LANES), lambda b, h, i: (b, h))
    return pl.pallas_call(
        functools.partial(_attn_kernel, tq=tq, lambda_init=lambda_init),
        out_shape=jax.ShapeDtypeStruct((t, w), BF16),
        grid=(batch, N_HEADS, nq),
        in_specs=[q_spec, kv_spec, kv_spec, _resident(lam.shape), _resident((1, LANES))],
        out_specs=q_spec,
        scratch_shapes=[pltpu.VMEM((2 * tq, LANES), BF16), pltpu.VMEM((2 * tq, LANES), F32),
                        pltpu.VMEM((2 * tq, LANES), F32), pltpu.VMEM((2 * tq, LANES), F32)],
        compiler_params=_params("arbitrary", "arbitrary", "arbitrary"),
        name="diff_attn",
    )(dq, dk, dv, lam, norm_g.reshape(1, LANES))


def _merge_kernel(x_ref, pu_ref, halo_ref, yb_ref, yc_ref, gate_ref, pbd_ref, ps_ref, pa_ref, pb_ref, pc_ref,
                  wo_ref, o_ref, *, tm, seq):
    d = x_ref.shape[1]
    t0 = (pl.program_id(0) * tm) % seq
    u = pu_ref[...]
    halo = jnp.where(t0 > 0, halo_ref[...], 0.0)
    e = jnp.concatenate([halo, u], axis=0)
    lane_grp = lax.broadcasted_iota(jnp.int32, u.shape, 1) // (u.shape[1] // len(POOL_WINDOWS))
    win = None
    shift = 1
    for g in range(len(POOL_WINDOWS)):
        e = e + pltpu.roll(e, shift, 0)
        shift *= 2
        cur = e[POOL_HALO:, :]
        win = cur if win is None else jnp.where(lane_grp >= g, cur, win)
    pos1 = t0 + 1 + lax.broadcasted_iota(jnp.int32, u.shape, 0)
    cnt = jnp.minimum(pos1, jnp.left_shift(2, lane_grp)).astype(F32)
    pooled = win / cnt - u
    ya = _dot(pooled.astype(BF16), pbd_ref[...]) * ps_ref[...]

    def gate(j):
        return _sigmoid(gate_ref[:, j * d:(j + 1) * d].astype(F32))

    merged = gate(0) * _dot(ya.astype(BF16), pa_ref[...])
    merged = merged + gate(1) * _dot(yb_ref[...], pb_ref[...])
    merged = merged + gate(2) * _dot(yc_ref[...], pc_ref[...])
    o_ref[...] = x_ref[...] + _dot(merged.astype(BF16), wo_ref[...])


def _merge(x, pool_u, yb, yc, gate, pool_bd, pool_scale, p_a, p_b, p_c, w_out, *, seq, tm=512):
    t, d = x.shape
    tm = min(tm, seq)
    cp = pool_u.shape[1]
    hb = tm // POOL_HALO

    def rows(width):
        return pl.BlockSpec((tm, width), lambda i: (i, 0))

    return pl.pallas_call(
        functools.partial(_merge_kernel, tm=tm, seq=seq),
        out_shape=jax.ShapeDtypeStruct((t, d), F32),
        grid=(t // tm,),
        in_specs=[rows(d), rows(cp), pl.BlockSpec((POOL_HALO, cp), lambda i: (jnp.maximum(i * hb - 1, 0), 0)),
                  rows(yb.shape[1]), rows(yc.shape[1]), rows(gate.shape[1]),
                  _resident(pool_bd.shape), _resident((1, cp)), _resident(p_a.shape), _resident(p_b.shape),
                  _resident(p_c.shape), _resident(w_out.shape)],
        out_specs=rows(d),
        compiler_params=_params("arbitrary"),
        name="merge",
    )(x, pool_u, pool_u, yb, yc, gate, pool_bd, pool_scale.reshape(1, cp), p_a, p_b, p_c, w_out)


def _rope_perm():
    half = HEAD_DIM // 2
    idx = []
    for h in range(N_HEADS):
        base = h * LANES
        for blk in (0, 2, 1, 3):
            idx.extend(range(base + blk * half, base + (blk + 1) * half))
    return np.asarray(idx)


def _prep_layer(w_in, pool_w):
    d = w_in.shape[0]
    n_pool = pool_w.shape[0] * pool_w.shape[1]
    mw = N_HEADS * HEAD_DIM
    sizes = (n_pool, mw, mw, mw, mw, 2 * N_HEADS, 2 * mw, 2 * mw, 2 * mw, 3 * d)
    offs = np.concatenate([[0], np.cumsum(sizes)])
    u_pool, m_q, m_k, m_v, m_o, m_if, d_q, d_k, d_v, gate = (w_in[:, offs[j]:offs[j + 1]] for j in range(len(sizes)))
    perm = _rope_perm()
    w_big = jnp.concatenate([m_q, m_k, m_v, m_o, u_pool, d_q[:, perm], d_k[:, perm], d_v, gate], axis=1).astype(BF16)
    w_ift = m_if.T.astype(BF16)
    pool_bd = jax.scipy.linalg.block_diag(*[pool_w[g] for g in range(pool_w.shape[0])]).astype(BF16)
    return w_big, w_ift, pool_bd


def kernel(x, positions, ffn1_norm, ffn1_w13, ffn1_w2, mix_norm, w_in, pool_w, pool_scale, m_conv_w, m_conv_b,
           m_gate_b, m_norm, d_lambda, d_norm, p_a, p_b, p_c, w_out, ffn2_norm, ffn2_w13, ffn2_w2, final_norm):
    batch, seq, d = x.shape
    depth = w_in.shape[0]
    cosf, sinf = _rope_tables(positions)
    h = x.reshape(batch * seq, d)
    for l in range(depth):
        lambda_init = 0.8 - 0.6 * math.exp(-0.3 * l)
        w_big, w_ift, pool_bd = _prep_layer(w_in[l], pool_w[l])
        h = _ffn(h, ffn1_norm[l], ffn1_w13[l].astype(BF16), ffn1_w2[l].astype(BF16))
        mqk, mv, mo, pool_u, ift, dq, dk, dv, gate = _inproj(h, mix_norm[l], w_big, w_ift, cosf, sinf)
        yb = _mlstm(mqk, mv, mo, ift, m_conv_w[l], m_conv_b[l], m_gate_b[l], m_norm[l], batch=batch)
        yc = _attn(dq, dk, dv, d_lambda[l], d_norm[l], lambda_init, batch=batch)
        h = _merge(h, pool_u, yb, yc, gate, pool_bd, pool_scale[l], p_a[l].astype(BF16), p_b[l].astype(BF16),
                   p_c[l].astype(BF16), w_out[l].astype(BF16), seq=seq)
        h = _ffn(h, ffn2_norm[l], ffn2_w13[l].astype(BF16), ffn2_w2[l].astype(BF16),
                 final_norm if l == depth - 1 else None)
    return h.reshape(batch, seq, d)
```

```python
import functools
import math

import jax
import jax.numpy as jnp
import numpy as np
from jax import lax
from jax.experimental import pallas as pl
from jax.experimental.pallas import tpu as pltpu

F32 = jnp.float32
BF16 = jnp.bfloat16
EPS = 1e-6
ROPE_THETA = 10000.0
POOL_WINDOWS = (2, 4, 8, 16)
N_HEADS = 4
HEAD_DIM = 64
LANES = 128
MLSTM_CHUNK = 128
LOG2E = math.log2(math.e)
POOL_HALO = 16
VMEM_LIMIT = 56 * 1024 * 1024


def _params(*sem):
    return pltpu.CompilerParams(dimension_semantics=sem, vmem_limit_bytes=VMEM_LIMIT)


def _resident(shape):
    zeros = (0,) * len(shape)
    return pl.BlockSpec(shape, lambda *_: zeros, pipeline_mode=pl.Buffered(1))


def _rms(x, g):
    return x * lax.rsqrt(jnp.mean(x * x, axis=-1, keepdims=True) + EPS) * g


def _sigmoid(x):
    return 1.0 / (1.0 + jnp.exp(-x))


def _dot(a, b):
    return jnp.dot(a, b, preferred_element_type=F32)


def _dot_nt(a, b):
    return lax.dot_general(a, b, (((1,), (1,)), ((), ())), preferred_element_type=F32)


def _dot_tn(a, b):
    return lax.dot_general(a, b, (((0,), (0,)), ((), ())), preferred_element_type=F32)


def _rope_kernel(pos_ref, inv_ref, cos_ref, sin_ref, nsin_ref):
    ang = pos_ref[...].astype(F32) * inv_ref[...]
    s = jnp.sin(ang)
    cos_ref[...] = jnp.cos(ang)
    sin_ref[...] = s
    nsin_ref[...] = -s


def _rope_tables(positions):
    t = positions.size
    half = HEAD_DIM // 2
    per_row = LANES // half
    rows = t // per_row
    pos = jnp.repeat(positions.reshape(rows, per_row), half, axis=1)
    inv = 1.0 / (ROPE_THETA ** (jnp.arange(0, HEAD_DIM, 2, dtype=F32) / HEAD_DIM))
    inv = jnp.tile(inv, per_row).reshape(1, LANES)
    tr = min(rows, 1024)
    spec = pl.BlockSpec((tr, LANES), lambda i: (i, 0))
    outs = pl.pallas_call(
        _rope_kernel,
        out_shape=[jax.ShapeDtypeStruct((rows, LANES), F32)] * 3,
        grid=(rows // tr,),
        in_specs=[spec, pl.BlockSpec((1, LANES), lambda i: (0, 0))],
        out_specs=[spec] * 3,
        compiler_params=_params("arbitrary"),
        name="rope_tables",
    )(pos, inv)
    cos, sin, nsin = (o.reshape(t, half) for o in outs)
    return jnp.concatenate([cos] * 4, axis=1), jnp.concatenate([nsin, nsin, sin, sin], axis=1)


def _ffn_kernel(x_ref, g_ref, w13_ref, w2_ref, *rest, d_ff, tf, final):
    if final:
        fg_ref, o_ref, acc_ref = rest
    else:
        o_ref, acc_ref = rest
    x = x_ref[...]
    xn = _rms(x, g_ref[...]).astype(BF16)
    for c in range(d_ff // tf):
        lo, hi = c * tf, (c + 1) * tf
        gate = _dot(xn, w13_ref[:, lo:hi])
        up = _dot(xn, w13_ref[:, d_ff + lo:d_ff + hi])
        act = (gate * _sigmoid(gate) * up).astype(BF16)
        part = _dot(act, w2_ref[lo:hi, :])
        if c == 0:
            acc_ref[...] = part
        else:
            acc_ref[...] += part
    y = x + 0.5 * acc_ref[...]
    if final:
        y = _rms(y, fg_ref[...])
    o_ref[...] = y


def _ffn(x, g, w13, w2, final_g=None, *, tm=512, tf=256):
    t, d = x.shape
    d_ff = w2.shape[0]
    tm = min(tm, t)
    final = final_g is not None
    row = pl.BlockSpec((tm, d), lambda i: (i, 0))
    in_specs = [row, _resident((1, d)), _resident((d, 2 * d_ff)), _resident((d_ff, d))]
    args = [x, g.reshape(1, d), w13, w2]
    if final:
        in_specs.append(_resident((1, d)))
        args.append(final_g.reshape(1, d))
    return pl.pallas_call(
        functools.partial(_ffn_kernel, d_ff=d_ff, tf=tf, final=final),
        out_shape=jax.ShapeDtypeStruct((t, d), F32),
        grid=(t // tm,),
        in_specs=in_specs,
        out_specs=row,
        scratch_shapes=[pltpu.VMEM((tm, d), F32)],
        compiler_params=_params("arbitrary"),
        name="ffn_final" if final else "ffn",
    )(*args)


_C_MQK, _C_MV, _C_MO, _C_POOL, _C_DQ, _C_DK, _C_DV, _C_GATE, _C_END = 0, 512, 768, 1024, 1280, 1792, 2304, 2816, 5888


def _inproj_kernel(x_ref, g_ref, w_ref, wift_ref, cos_ref, sin_ref,
                   mqk_ref, mv_ref, mo_ref, pool_ref, ift_ref, dq_ref, dk_ref, dv_ref, gate_ref):
    xn = _rms(x_ref[...], g_ref[...]).astype(BF16)

    def proj(lo, hi):
        return _dot(xn, w_ref[:, lo:hi])

    mqk_ref[...] = proj(_C_MQK, _C_MV)
    mv_ref[...] = proj(_C_MV, _C_MO)
    mo_ref[...] = proj(_C_MO, _C_POOL)
    pool_ref[...] = proj(_C_POOL, _C_DQ)
    ift_ref[...] = _dot_nt(wift_ref[...], xn)
    cos = cos_ref[...]
    sin = sin_ref[...]
    for ref, lo, scale in ((dq_ref, _C_DQ, HEAD_DIM ** -0.5 * LOG2E), (dk_ref, _C_DK, 1.0)):
        z = proj(lo, lo + N_HEADS * LANES)
        for h in range(N_HEADS):
            zh = z[:, h * LANES:(h + 1) * LANES]
            r = zh * cos + pltpu.roll(zh, LANES // 2, 1) * sin
            ref[:, h * LANES:(h + 1) * LANES] = (r * scale).astype(BF16)
    dv_ref[...] = proj(_C_DV, _C_GATE).astype(BF16)
    gate_ref[...] = proj(_C_GATE, _C_END).astype(BF16)


def _inproj(x, g, w_big, w_ift, cosf, sinf, *, tm=512):
    t, d = x.shape
    tm = min(tm, t)
    widths = (512, 256, 256, 256)

    def rows(width):
        return pl.BlockSpec((tm, width), lambda i: (i, 0))

    out_shape = [jax.ShapeDtypeStruct((t, w), F32) for w in widths]
    out_specs = [rows(w) for w in widths]
    out_shape.append(jax.ShapeDtypeStruct((8, t), F32))
    out_specs.append(pl.BlockSpec((8, tm), lambda i: (0, i)))
    for w in (512, 512, 512, 3072):
        out_shape.append(jax.ShapeDtypeStruct((t, w), BF16))
        out_specs.append(rows(w))
    return pl.pallas_call(
        _inproj_kernel,
        out_shape=out_shape,
        grid=(t // tm,),
        in_specs=[rows(d), _resident((1, d)), _resident(w_big.shape), _resident(w_ift.shape),
                  rows(LANES), rows(LANES)],
        out_specs=out_specs,
        compiler_params=_params("arbitrary"),
        name="inproj",
    )(x, g.reshape(1, d), w_big, w_ift, cosf, sinf)


def _head_of_lane(shape, dim):
    return lax.broadcasted_iota(jnp.int32, shape, dim) // HEAD_DIM


def _per_head_to_lanes(cols, lane_head):
    out = cols[N_HEADS - 1]
    for h in range(N_HEADS - 2, -1, -1):
        out = jnp.where(lane_head == h, cols[h], out)
    return out


def _mlstm_kernel(qk_ref, v_ref, o_ref, ift_ref, cw_ref, cb_ref, gb_ref, ng_ref, y_ref,
                  ext_ref, ct_ref, n_ref, m_ref, *, tb):
    L = MLSTM_CHUNK
    W = N_HEADS * HEAD_DIM
    kconv = cw_ref.shape[0]

    @pl.when(pl.program_id(1) == 0)
    def _():
        ext_ref[0:8, :] = jnp.zeros((8, 2 * W), F32)
        ct_ref[...] = jnp.zeros_like(ct_ref)
        n_ref[...] = jnp.zeros_like(n_ref)
        m_ref[...] = jnp.zeros_like(m_ref)

    ext_ref[8:8 + tb, :] = qk_ref[...]
    conv = cb_ref[...] + cw_ref[kconv - 1:kconv, :] * ext_ref[8:8 + tb, :]
    for j in range(kconv - 1):
        off = 8 - (kconv - 1) + j
        conv = conv + cw_ref[j:j + 1, :] * ext_ref[off:off + tb, :]
    ext_ref[0:8, :] = ext_ref[tb:tb + 8, :]
    qk = conv * _sigmoid(conv)
    q_all = qk[:, :W]
    k_all = qk[:, W:] * (HEAD_DIM ** -0.5)

    lane_head = _head_of_lane((L, W), 1)
    lane_head_row = _head_of_lane((1, W), 1)
    tt = lax.broadcasted_iota(jnp.int32, (L, L), 0)
    ss = lax.broadcasted_iota(jnp.int32, (L, L), 1)
    causal = ss <= tt
    diag = ss == tt
    lane8 = lax.broadcasted_iota(jnp.int32, (8, L), 1)
    bd_mask = _head_of_lane((W, W), 0) == _head_of_lane((W, W), 1)

    ct = ct_ref[...]
    nrow = n_ref[...]
    m = [m_ref[h:h + 1, 0:1] for h in range(N_HEADS)]

    for c in range(tb // L):
        r0 = c * L
        q = q_all[r0:r0 + L, :]
        k = k_all[r0:r0 + L, :]
        v = v_ref[r0:r0 + L, :]
        vb = v.astype(BF16)
        gates = ift_ref[:, r0:r0 + L] + gb_ref[...]
        logf = jnp.minimum(gates, 0.0) - jnp.log1p(jnp.exp(-jnp.abs(gates)))
        bcum = logf
        for sh in (1, 2, 4, 8, 16, 32, 64):
            bcum = bcum + jnp.where(lane8 >= sh, pltpu.roll(bcum, sh, 1), 0.0)

        qm = [jnp.where(lane_head == h, q, 0.0) for h in range(N_HEADS)]
        s_all = _dot_nt(jnp.concatenate(qm, axis=0).astype(BF16), k.astype(BF16))
        bcol, rmax, den0, wk0, blast, gmax, sc0 = [], [], [], [], [], [], []
        for h in range(N_HEADS):
            i_row = gates[h:h + 1, :]
            lf_row = logf[N_HEADS + h:N_HEADS + h + 1, :]
            b_row = bcum[N_HEADS + h:N_HEADS + h + 1, :]
            b_col = jnp.sum(jnp.where(causal, lf_row, 0.0), axis=-1, keepdims=True)
            i_col = jnp.sum(jnp.where(diag, i_row, 0.0), axis=-1, keepdims=True)
            dm = jnp.where(causal, b_col + (i_row - b_row), -jnp.inf)
            r_max = jnp.max(dm, axis=-1, keepdims=True)
            sc = s_all[h * L:(h + 1) * L, :] * jnp.exp(dm - r_max)
            b_last = jnp.sum(lf_row, axis=-1, keepdims=True)
            g_row = b_last - b_row + i_row
            g_max = jnp.max(g_row, axis=-1, keepdims=True)
            bcol.append(b_col)
            rmax.append(r_max)
            den0.append(jnp.sum(sc, axis=-1, keepdims=True))
            wk0.append(jnp.exp(b_last - b_col + i_col - g_max))
            blast.append(b_last)
            gmax.append(g_max)
            sc0.append(sc)
        r_all = _dot(jnp.concatenate(sc0, axis=0).astype(BF16), vb)
        kw = k * _per_head_to_lanes(wk0, lane_head)
        dct = jnp.where(bd_mask, _dot_tn(kw.astype(BF16), vb), 0.0)
        dn = jnp.sum(kw, axis=0, keepdims=True)

        qc = _dot(q.astype(BF16), ct.astype(BF16))
        qn = q * nrow
        z, decay, grow = [], [], []
        for h in range(N_HEADS):
            inter = bcol[h] + m[h]
            m_t = jnp.maximum(inter, rmax[h])
            a = jnp.exp(rmax[h] - m_t)
            b = jnp.exp(inter - m_t)
            qn_h = jnp.sum(jnp.where(lane_head == h, qn, 0.0), axis=-1, keepdims=True)
            den = a * den0[h] + b * qn_h
            inv = 1.0 / jnp.maximum(jnp.abs(den), jnp.exp(-m_t))
            z.append((a * inv) * r_all[h * L:(h + 1) * L, :] + (b * inv) * qc)
            m_new = jnp.maximum(blast[h] + m[h], gmax[h])
            decay.append(jnp.exp(blast[h] + m[h] - m_new))
            grow.append(jnp.exp(gmax[h] - m_new))
            m[h] = m_new
        hout = _per_head_to_lanes(z, lane_head)
        decay_row = _per_head_to_lanes(decay, lane_head_row)
        grow_row = _per_head_to_lanes(grow, lane_head_row)
        ct = decay_row * ct + grow_row * dct
        nrow = decay_row * nrow + grow_row * dn

        mu = [jnp.sum(jnp.where(lane_head == h, hout, 0.0), axis=-1, keepdims=True) for h in range(N_HEADS)]
        dlt = hout - _per_head_to_lanes(mu, lane_head) * (1.0 / HEAD_DIM)
        var = [jnp.sum(jnp.where(lane_head == h, dlt * dlt, 0.0), axis=-1, keepdims=True) for h in range(N_HEADS)]
        rstd = lax.rsqrt(_per_head_to_lanes(var, lane_head) * (1.0 / HEAD_DIM) + EPS)
        y = _sigmoid(o_ref[r0:r0 + L, :]) * (dlt * rstd * ng_ref[...])
        y_ref[r0:r0 + L, :] = y.astype(y_ref.dtype)

    ct_ref[...] = ct
    n_ref[...] = nrow
    for h in range(N_HEADS):
        m_ref[h:h + 1, :] = jnp.broadcast_to(m[h], (1, LANES))


def _mlstm(mqk, mv, mo, ift, conv_w, conv_b, gate_b, norm_g, *, batch, tb=512):
    t, w2 = mqk.shape
    w = w2 // 2
    s = t // batch
    tb = min(tb, s)
    nb = s // tb

    def rows(width):
        return pl.BlockSpec((tb, width), lambda b, i: (b * nb + i, 0))

    return pl.pallas_call(
        functools.partial(_mlstm_kernel, tb=tb),
        out_shape=jax.ShapeDtypeStruct((t, w), BF16),
        grid=(batch, nb),
        in_specs=[rows(w2), rows(w), rows(w), pl.BlockSpec((8, tb), lambda b, i: (0, b * nb + i)),
                  _resident(conv_w.shape), _resident((1, w2)), _resident((8, LANES)), _resident((1, w))],
        out_specs=rows(w),
        scratch_shapes=[pltpu.VMEM((tb + 8, w2), F32), pltpu.VMEM((w, w), F32),
                        pltpu.VMEM((1, w), F32), pltpu.VMEM((8, LANES), F32)],
        compiler_params=_params("arbitrary", "arbitrary"),
        name="mlstm",
    )(mqk, mv, mo, ift, conv_w, conv_b.reshape(1, w2),
      jnp.broadcast_to(gate_b.reshape(8, 1), (8, LANES)), norm_g.reshape(1, w))


def _attn_kernel(ti_ref, tj_ref, q_ref, k_ref, v_ref, lam_ref, ng_ref, o_ref, qs_ref, va_ref, m_ref, acc_ref,
                 s0_ref, s1_ref, *, tq, rc, n_off, n_diag, lambda_init):
    nq = q_ref.shape[0] // tq
    comp_a = (lax.broadcasted_iota(jnp.int32, (tq, LANES), 1) % HEAD_DIM) < HEAD_DIM // 2

    def stack_q(i, carry):
        q = q_ref[pl.ds(pl.multiple_of(i * tq, tq), tq), :]
        zero = jnp.zeros_like(q)
        base = pl.multiple_of(i * (2 * tq), 2 * tq)
        qs_ref[pl.ds(base, tq), :] = jnp.where(comp_a, q, zero)
        qs_ref[pl.ds(base + tq, tq), :] = jnp.where(comp_a, zero, q)
        return carry

    lax.fori_loop(0, nq, stack_q, 0)
    va_ref[:, 0:LANES] = v_ref[...]
    va_ref[:, LANES:] = jnp.ones((va_ref.shape[0], LANES), va_ref.dtype)
    m_ref[...] = jnp.full_like(m_ref, -jnp.inf)
    acc_ref[...] = jnp.zeros_like(acc_ref)

    def scores(buf_ref, t):
        q_row = pl.multiple_of(ti_ref[t] * (2 * tq), 2 * tq)
        k_row = pl.multiple_of(tj_ref[t] * tq, tq)
        buf_ref[...] = _dot_nt(qs_ref[pl.ds(q_row, 2 * tq), :], k_ref[pl.ds(k_row, tq), :])

    def update(buf_ref, t, diagonal):
        q_row = ti_ref[t] * (2 * tq)
        k_row = pl.multiple_of(tj_ref[t] * tq, tq)
        for r in range(2 * tq // rc):
            rows = pl.ds(pl.multiple_of(q_row + r * rc, rc), rc)
            first = (r * rc) % tq
            nk = first + rc if diagonal else tq
            s = buf_ref[r * rc:(r + 1) * rc, 0:nk]
            if diagonal:
                row = first + lax.broadcasted_iota(jnp.int32, s.shape, 0)
                col = lax.broadcasted_iota(jnp.int32, s.shape, 1)
                s = jnp.where(col <= row, s, -jnp.inf)
            m_prev = m_ref[rows, :]
            m_next = jnp.maximum(m_prev, jnp.max(s, axis=-1, keepdims=True))
            alpha = jnp.exp2(m_prev - m_next)
            p = jnp.exp2(s - pltpu.repeat(m_next, nk // LANES, 1))
            pv = _dot(p.astype(BF16), va_ref[pl.ds(k_row, nk), :])
            acc_ref[rows, :] = pltpu.repeat(alpha, 2, 1) * acc_ref[rows, :] + pv
            m_ref[rows, :] = m_next

    bufs = (s0_ref, s1_ref)

    def run(t0, count, diagonal, cur):
        a_ref, b_ref = bufs[cur], bufs[1 - cur]

        def pair(pp, carry):
            t = t0 + 2 * pp
            scores(b_ref, t + 1)
            update(a_ref, t, diagonal)
            scores(a_ref, t + 2)
            update(b_ref, t + 1, diagonal)
            return carry

        lax.fori_loop(0, count // 2, pair, 0)
        if count % 2:
            t = t0 + count - 1
            scores(b_ref, t + 1)
            update(a_ref, t, diagonal)
            cur = 1 - cur
        return cur

    scores(s0_ref, 0)
    cur = run(0, n_off, False, 0)
    run(n_off, n_diag, True, cur)

    lam = lam_ref[...]
    lam_full = (jnp.exp(jnp.sum(lam[0:1, :] * lam[1:2, :], axis=-1, keepdims=True))
                - jnp.exp(jnp.sum(lam[2:3, :] * lam[3:4, :], axis=-1, keepdims=True)) + lambda_init)

    def finish(i, carry):
        base = pl.multiple_of(i * (2 * tq), 2 * tq)
        acc_a = acc_ref[pl.ds(base, tq), :]
        acc_b = acc_ref[pl.ds(base + tq, tq), :]
        out = acc_a[:, 0:LANES] / acc_a[:, LANES:] - lam_full * (acc_b[:, 0:LANES] / acc_b[:, LANES:])
        out = out * lax.rsqrt(jnp.mean(out * out, axis=-1, keepdims=True) + EPS) * ng_ref[...] * (1.0 - lambda_init)
        o_ref[pl.ds(pl.multiple_of(i * tq, tq), tq), :] = out.astype(o_ref.dtype)
        return carry

    lax.fori_loop(0, nq, finish, 0)


def _attn(dq, dk, dv, lam, norm_g, lambda_init, *, batch, tq=512, rc=256):
    t, w = dq.shape
    s = t // batch
    tq = min(tq, s)
    nq = s // tq
    below = [(i, j) for j in range(nq) for i in range(j + 1, nq)]
    pairs = below + [(i, i) for i in range(nq)] + [(0, 0)]
    tab_i = jnp.asarray([p[0] for p in pairs], jnp.int32)
    tab_j = jnp.asarray([p[1] for p in pairs], jnp.int32)
    head = pl.BlockSpec((s, LANES), lambda b, h, ti, tj: (b, h))
    grid_spec = pltpu.PrefetchScalarGridSpec(
        num_scalar_prefetch=2,
        grid=(batch, N_HEADS),
        in_specs=[head, head, head,
                  pl.BlockSpec(lam.shape, lambda b, h, ti, tj: (0, 0)),
                  pl.BlockSpec((1, LANES), lambda b, h, ti, tj: (0, 0))],
        out_specs=head,
        scratch_shapes=[pltpu.VMEM((2 * s, LANES), BF16), pltpu.VMEM((s, 2 * LANES), BF16),
                        pltpu.VMEM((2 * s, LANES), F32), pltpu.VMEM((2 * s, 2 * LANES), F32),
                        pltpu.VMEM((2 * tq, tq), F32), pltpu.VMEM((2 * tq, tq), F32)],
    )
    return pl.pallas_call(
        functools.partial(_attn_kernel, tq=tq, rc=rc, n_off=len(below), n_diag=nq, lambda_init=lambda_init),
        out_shape=jax.ShapeDtypeStruct((t, w), BF16),
        grid_spec=grid_spec,
        compiler_params=_params("arbitrary", "arbitrary"),
        name="diff_attn",
    )(tab_i, tab_j, dq, dk, dv, lam, norm_g.reshape(1, LANES))


def _merge_kernel(x_ref, pu_ref, halo_ref, yb_ref, yc_ref, gate_ref, pbd_ref, ps_ref, pa_ref, pb_ref, pc_ref,
                  wo_ref, o_ref, *, tm, seq):
    d = x_ref.shape[1]
    t0 = (pl.program_id(0) * tm) % seq
    u = pu_ref[...]
    halo = jnp.where(t0 > 0, halo_ref[...], 0.0)
    e = jnp.concatenate([halo, u], axis=0)
    lane_grp = lax.broadcasted_iota(jnp.int32, u.shape, 1) // (u.shape[1] // len(POOL_WINDOWS))
    win = None
    shift = 1
    for g in range(len(POOL_WINDOWS)):
        e = e + pltpu.roll(e, shift, 0)
        shift *= 2
        cur = e[POOL_HALO:, :]
        win = cur if win is None else jnp.where(lane_grp >= g, cur, win)
    pos1 = t0 + 1 + lax.broadcasted_iota(jnp.int32, u.shape, 0)
    cnt = jnp.minimum(pos1, jnp.left_shift(2, lane_grp)).astype(F32)
    pooled = win / cnt - u
    ya = _dot(pooled.astype(BF16), pbd_ref[...]) * ps_ref[...]

    def gate(j):
        return _sigmoid(gate_ref[:, j * d:(j + 1) * d].astype(F32))

    merged = gate(0) * _dot(ya.astype(BF16), pa_ref[...])
    merged = merged + gate(1) * _dot(yb_ref[...], pb_ref[...])
    merged = merged + gate(2) * _dot(yc_ref[...], pc_ref[...])
    o_ref[...] = x_ref[...] + _dot(merged.astype(BF16), wo_ref[...])


def _merge(x, pool_u, yb, yc, gate, pool_bd, pool_scale, p_a, p_b, p_c, w_out, *, seq, tm=512):
    t, d = x.shape
    tm = min(tm, seq)
    cp = pool_u.shape[1]
    hb = tm // POOL_HALO

    def rows(width):
        return pl.BlockSpec((tm, width), lambda i: (i, 0))

    return pl.pallas_call(
        functools.partial(_merge_kernel, tm=tm, seq=seq),
        out_shape=jax.ShapeDtypeStruct((t, d), F32),
        grid=(t // tm,),
        in_specs=[rows(d), rows(cp), pl.BlockSpec((POOL_HALO, cp), lambda i: (jnp.maximum(i * hb - 1, 0), 0)),
                  rows(yb.shape[1]), rows(yc.shape[1]), rows(gate.shape[1]),
                  _resident(pool_bd.shape), _resident((1, cp)), _resident(p_a.shape), _resident(p_b.shape),
                  _resident(p_c.shape), _resident(w_out.shape)],
        out_specs=rows(d),
        compiler_params=_params("arbitrary"),
        name="merge",
    )(x, pool_u, pool_u, yb, yc, gate, pool_bd, pool_scale.reshape(1, cp), p_a, p_b, p_c, w_out)


def _rope_perm():
    half = HEAD_DIM // 2
    idx = []
    for h in range(N_HEADS):
        base = h * LANES
        for blk in (0, 2, 1, 3):
            idx.extend(range(base + blk * half, base + (blk + 1) * half))
    return np.asarray(idx)


def _prep_layer(w_in, pool_w):
    d = w_in.shape[0]
    n_pool = pool_w.shape[0] * pool_w.shape[1]
    mw = N_HEADS * HEAD_DIM
    sizes = (n_pool, mw, mw, mw, mw, 2 * N_HEADS, 2 * mw, 2 * mw, 2 * mw, 3 * d)
    offs = np.concatenate([[0], np.cumsum(sizes)])
    u_pool, m_q, m_k, m_v, m_o, m_if, d_q, d_k, d_v, gate = (w_in[:, offs[j]:offs[j + 1]] for j in range(len(sizes)))
    perm = _rope_perm()
    w_big = jnp.concatenate([m_q, m_k, m_v, m_o, u_pool, d_q[:, perm], d_k[:, perm], d_v, gate], axis=1).astype(BF16)
    w_ift = m_if.T.astype(BF16)
    pool_bd = jax.scipy.linalg.block_diag(*[pool_w[g] for g in range(pool_w.shape[0])]).astype(BF16)
    return w_big, w_ift, pool_bd


def kernel(x, positions, ffn1_norm, ffn1_w13, ffn1_w2, mix_norm, w_in, pool_w, pool_scale, m_conv_w, m_conv_b,
           m_gate_b, m_norm, d_lambda, d_norm, p_a, p_b, p_c, w_out, ffn2_norm, ffn2_w13, ffn2_w2, final_norm):
    batch, seq, d = x.shape
    depth = w_in.shape[0]
    cosf, sinf = _rope_tables(positions)
    h = x.reshape(batch * seq, d)
    for l in range(depth):
        lambda_init = 0.8 - 0.6 * math.exp(-0.3 * l)
        w_big, w_ift, pool_bd = _prep_layer(w_in[l], pool_w[l])
        h = _ffn(h, ffn1_norm[l], ffn1_w13[l].astype(BF16), ffn1_w2[l].astype(BF16))
        mqk, mv, mo, pool_u, ift, dq, dk, dv, gate = _inproj(h, mix_norm[l], w_big, w_ift, cosf, sinf)
        yb = _mlstm(mqk, mv, mo, ift, m_conv_w[l], m_conv_b[l], m_gate_b[l], m_norm[l], batch=batch)
        yc = _attn(dq, dk, dv, d_lambda[l], d_norm[l], lambda_init, batch=batch)
        h = _merge(h, pool_u, yb, yc, gate, pool_bd, pool_scale[l], p_a[l].astype(BF16), p_b[l].astype(BF16),
                   p_c[l].astype(BF16), w_out[l].astype(BF16), seq=seq)
        h = _ffn(h, ffn2_norm[l], ffn2_w13[l].astype(BF16), ffn2_w2[l].astype(BF16),
                 final_norm if l == depth - 1 else None)
    return h.reshape(batch, seq, d)
```

```python
import functools
import math

import jax
import jax.numpy as jnp
import numpy as np
from jax import lax
from jax.experimental import pallas as pl
from jax.experimental.pallas import tpu as pltpu

F32 = jnp.float32
BF16 = jnp.bfloat16
EPS = 1e-6
ROPE_THETA = 10000.0
POOL_WINDOWS = (2, 4, 8, 16)
N_HEADS = 4
HEAD_DIM = 64
LANES = 128
MLSTM_CHUNK = 128
LOG2E = math.log2(math.e)
POOL_HALO = 16
VMEM_LIMIT = 56 * 1024 * 1024


def _params(*sem):
    return pltpu.CompilerParams(dimension_semantics=sem, vmem_limit_bytes=VMEM_LIMIT)


def _resident(shape):
    zeros = (0,) * len(shape)
    return pl.BlockSpec(shape, lambda *_: zeros, pipeline_mode=pl.Buffered(1))


def _on_lanes(a):
    return jnp.broadcast_to(a[..., None], a.shape + (LANES,))


def _lane_scan(x, lane, combine, fill):
    sh = 1
    while sh < x.shape[1]:
        x = combine(x, jnp.where(lane >= sh, pltpu.roll(x, sh, 1), fill))
        sh *= 2
    return x


def _rms(x, g):
    return x * lax.rsqrt(jnp.mean(x * x, axis=-1, keepdims=True) + EPS) * g


def _sigmoid(x):
    return 1.0 / (1.0 + jnp.exp(-x))


def _dot(a, b):
    return jnp.dot(a, b, preferred_element_type=F32)


def _dot_nt(a, b):
    return lax.dot_general(a, b, (((1,), (1,)), ((), ())), preferred_element_type=F32)


def _rope_kernel(pos_ref, inv_ref, cos_ref, sin_ref, nsin_ref):
    ang = pos_ref[...].astype(F32) * inv_ref[...]
    s = jnp.sin(ang)
    cos_ref[...] = jnp.cos(ang)
    sin_ref[...] = s
    nsin_ref[...] = -s


def _rope_tables(positions):
    t = positions.size
    half = HEAD_DIM // 2
    per_row = LANES // half
    rows = t // per_row
    pos = jnp.repeat(positions.reshape(rows, per_row), half, axis=1)
    inv = 1.0 / (ROPE_THETA ** (jnp.arange(0, HEAD_DIM, 2, dtype=F32) / HEAD_DIM))
    inv = jnp.tile(inv, per_row).reshape(1, LANES)
    tr = min(rows, 1024)
    spec = pl.BlockSpec((tr, LANES), lambda i: (i, 0))
    outs = pl.pallas_call(
        _rope_kernel,
        out_shape=[jax.ShapeDtypeStruct((rows, LANES), F32)] * 3,
        grid=(rows // tr,),
        in_specs=[spec, pl.BlockSpec((1, LANES), lambda i: (0, 0))],
        out_specs=[spec] * 3,
        compiler_params=_params("arbitrary"),
        name="rope_tables",
    )(pos, inv)
    cos, sin, nsin = (o.reshape(rows, per_row, 1, half) for o in outs)
    cosf = jnp.broadcast_to(cos, (rows, per_row, 4, half)).reshape(t, LANES)
    sinf = jnp.concatenate([nsin, nsin, sin, sin], axis=2).reshape(t, LANES)
    return cosf, sinf


def _ffn_kernel(x_ref, g_ref, w13_ref, w2_ref, *rest, d_ff, tf, final):
    if final:
        fg_ref, o_ref, acc_ref = rest
    else:
        o_ref, acc_ref = rest
    x = x_ref[...]
    xn = _rms(x, g_ref[...]).astype(BF16)
    for c in range(d_ff // tf):
        lo, hi = c * tf, (c + 1) * tf
        gate = _dot(xn, w13_ref[:, lo:hi])
        up = _dot(xn, w13_ref[:, d_ff + lo:d_ff + hi])
        act = (gate * _sigmoid(gate) * up).astype(BF16)
        part = _dot(act, w2_ref[lo:hi, :])
        if c == 0:
            acc_ref[...] = part
        else:
            acc_ref[...] += part
    y = x + 0.5 * acc_ref[...]
    if final:
        y = _rms(y, fg_ref[...])
    o_ref[...] = y


def _ffn(x, g, w13, w2, layer, final_g=None, *, tm=512, tf=256):
    t, d = x.shape
    d_ff = w2.shape[1]
    tm = min(tm, t)
    final = final_g is not None
    row = pl.BlockSpec((tm, d), lambda i: (i, 0))

    def weight(shape):
        return pl.BlockSpec((None,) + shape, lambda i: (layer, 0, 0), pipeline_mode=pl.Buffered(1))

    in_specs = [row, _resident((1, d)), weight((d, 2 * d_ff)), weight((d_ff, d))]
    args = [x, g.reshape(1, d), w13, w2]
    if final:
        in_specs.append(_resident((1, d)))
        args.append(final_g.reshape(1, d))
    return pl.pallas_call(
        functools.partial(_ffn_kernel, d_ff=d_ff, tf=tf, final=final),
        out_shape=jax.ShapeDtypeStruct((t, d), F32),
        grid=(t // tm,),
        in_specs=in_specs,
        out_specs=row,
        scratch_shapes=[pltpu.VMEM((tm, d), F32)],
        compiler_params=_params("arbitrary"),
        name="ffn_final" if final else "ffn",
    )(*args)


_C_MO, _C_POOL, _C_DQ, _C_DK, _C_DV, _C_GATE, _C_END = 0, 256, 512, 1024, 1536, 2048, 5120
_R_IF, _R_QK, _R_V, _R_END = 0, 8, 520, 776


def _inproj_kernel(x_ref, g_ref, w_ref, wt_ref, cw_ref, cb_ref, gb_ref, cos_ref, sin_ref,
                   mq_ref, mk_ref, mv_ref, mo_ref, pool_ref, mg_ref, dq_ref, dk_ref, dv_ref, gate_ref,
                   ext_ref, *, blocks_per_seq):
    tm = x_ref.shape[0]
    halo = ext_ref.shape[1] - tm
    kconv = cw_ref.shape[0]
    mw = mq_ref.shape[0]
    xn = _rms(x_ref[...], g_ref[...]).astype(BF16)

    def proj(lo, hi):
        return _dot(xn, w_ref[:, lo:hi])

    zt = _dot_nt(wt_ref[...], xn)
    mv_ref[...] = zt[_R_V:_R_END, :].astype(BF16)

    lane8 = lax.broadcasted_iota(jnp.int32, (8, MLSTM_CHUNK), 1)
    for c in range(tm // MLSTM_CHUNK):
        sl = slice(c * MLSTM_CHUNK, (c + 1) * MLSTM_CHUNK)
        g8 = zt[_R_IF:_R_QK, sl] + gb_ref[...]
        logf = pltpu.roll(jnp.minimum(g8, 0.0) - jnp.log1p(jnp.exp(-jnp.abs(g8))), N_HEADS, 0)
        bcum = _lane_scan(logf, lane8, jnp.add, 0.0)
        cc = g8 - bcum
        mg_ref[0:8, sl] = bcum
        mg_ref[8:16, sl] = cc
        mg_ref[16:24, sl] = jnp.broadcast_to(jnp.sum(logf, axis=1, keepdims=True), cc.shape)
        mg_ref[24:32, sl] = jnp.broadcast_to(jnp.max(cc, axis=1, keepdims=True), cc.shape)

    @pl.when(pl.program_id(0) % blocks_per_seq == 0)
    def _():
        ext_ref[:, 0:halo] = jnp.zeros((ext_ref.shape[0], halo), F32)

    ext_ref[:, halo:halo + tm] = zt[_R_QK:_R_V, :]
    ext = ext_ref[...]
    taps = [ext if j == kconv - 1 else pltpu.roll(ext, kconv - 1 - j, 1) for j in range(kconv)]
    ext_ref[:, 0:halo] = ext_ref[:, tm:tm + halo]
    for c in range(tm // LANES):
        conv = cb_ref[...]
        for j in range(kconv):
            conv = conv + cw_ref[j] * taps[j][:, halo + c * LANES:halo + (c + 1) * LANES]
        qk = conv * _sigmoid(conv)
        mq_ref[:, c * LANES:(c + 1) * LANES] = qk[0:mw, :].astype(BF16)
        mk_ref[c * LANES:(c + 1) * LANES, :] = jnp.transpose(qk[mw:2 * mw, :] * (HEAD_DIM ** -0.5)).astype(BF16)
    mo_ref[...] = proj(_C_MO, _C_POOL)
    pool_ref[...] = proj(_C_POOL, _C_DQ)
    cos = cos_ref[...]
    sin = sin_ref[...]
    for ref, lo, scale in ((dq_ref, _C_DQ, HEAD_DIM ** -0.5 * LOG2E), (dk_ref, _C_DK, 1.0)):
        z = proj(lo, lo + N_HEADS * LANES)
        for h in range(N_HEADS):
            zh = z[:, h * LANES:(h + 1) * LANES]
            r = zh * cos + pltpu.roll(zh, LANES // 2, 1) * sin
            ref[:, h * LANES:(h + 1) * LANES] = (r * scale).astype(BF16)
    dv_ref[...] = proj(_C_DV, _C_GATE).astype(BF16)
    gate_ref[...] = proj(_C_GATE, _C_END).astype(BF16)


def _inproj(x, g, w_big, w_t, conv_w, conv_b, gate_b, cosf, sinf, *, seq, tm=512):
    t, d = x.shape
    tm = min(tm, seq)
    mw = (_R_V - _R_QK) // 2
    kconv = conv_w.shape[0]

    def rows(width):
        return pl.BlockSpec((tm, width), lambda i: (i, 0))

    def cols(height):
        return pl.BlockSpec((height, tm), lambda i: (0, i))

    out_shape = [jax.ShapeDtypeStruct((mw, t), BF16), jax.ShapeDtypeStruct((t, mw), BF16),
                 jax.ShapeDtypeStruct((mw, t), BF16)]
    out_specs = [cols(mw), rows(mw), cols(mw)]
    for width in (_C_POOL - _C_MO, _C_DQ - _C_POOL):
        out_shape.append(jax.ShapeDtypeStruct((t, width), F32))
        out_specs.append(rows(width))
    out_shape.append(jax.ShapeDtypeStruct((4 * (_R_QK - _R_IF), t), F32))
    out_specs.append(cols(4 * (_R_QK - _R_IF)))
    for w in (512, 512, 512, 3072):
        out_shape.append(jax.ShapeDtypeStruct((t, w), BF16))
        out_specs.append(rows(w))
    return pl.pallas_call(
        functools.partial(_inproj_kernel, blocks_per_seq=seq // tm),
        out_shape=out_shape,
        grid=(t // tm,),
        in_specs=[rows(d), _resident((1, d)), _resident(w_big.shape), _resident(w_t.shape),
                  _resident((kconv, 2 * mw, LANES)), _resident((2 * mw, LANES)), _resident((8, LANES)),
                  rows(LANES), rows(LANES)],
        out_specs=out_specs,
        scratch_shapes=[pltpu.VMEM((2 * mw, tm + LANES), F32)],
        compiler_params=_params("arbitrary"),
        name="inproj",
    )(x, g.reshape(1, d), w_big, w_t, _on_lanes(conv_w), _on_lanes(conv_b), _on_lanes(gate_b), cosf, sinf)


def _head_of_lane(shape, dim):
    return lax.broadcasted_iota(jnp.int32, shape, dim) // HEAD_DIM


def _mlstm_kernel(q_ref, k_ref, v_ref, o_ref, g_ref, ng_ref, y_ref, c_ref, n_ref, m_ref, *, tb):
    L = MLSTM_CHUNK
    H, D = N_HEADS, HEAD_DIM
    W = H * D

    @pl.when(pl.program_id(1) == 0)
    def _():
        c_ref[...] = jnp.zeros_like(c_ref)
        n_ref[...] = jnp.zeros_like(n_ref)
        m_ref[...] = jnp.zeros_like(m_ref)

    row8 = lax.broadcasted_iota(jnp.int32, (8, L), 0)
    ss = lax.broadcasted_iota(jnp.int32, (L, L), 0)
    tt = lax.broadcasted_iota(jnp.int32, (L, L), 1)
    causal = ss <= tt
    diag = ss == tt
    lane_head = _head_of_lane((L, W), 1)
    bd_mask = _head_of_lane((W, W), 0) == _head_of_lane((W, W), 1)
    n_mask = lax.broadcasted_iota(jnp.int32, (8, W), 0) == _head_of_lane((8, W), 1)
    ones8 = jnp.ones((8, L), BF16)

    def head_rows(x8):
        col = jnp.concatenate([jnp.broadcast_to(x8[h:h + 1, :], (D, L)) for h in range(H)], axis=0)
        return jnp.concatenate([col] * (W // L), axis=1)

    def wide(x8):
        return jnp.concatenate([x8] * (W // L), axis=1)

    cmat = c_ref[...]
    n8 = n_ref[...]
    m8 = m_ref[...]

    for c in range(tb // L):
        q_t = q_ref[:, c * L:(c + 1) * L]
        k_n = k_ref[c * L:(c + 1) * L, :]
        v_t = v_ref[:, c * L:(c + 1) * L]

        bcum = g_ref[0:8, c * L:(c + 1) * L]
        cc = g_ref[8:16, c * L:(c + 1) * L]
        blast = g_ref[16:24, c * L:(c + 1) * L]
        call = g_ref[24:32, c * L:(c + 1) * L]
        gmax = blast + call
        wk = jnp.exp(cc - call)

        k_stack = jnp.concatenate([jnp.where(lane_head == h, k_n, jnp.zeros_like(k_n)) for h in range(H)], axis=0)
        st_all = _dot(k_stack, q_t)
        r_t, den0, cmax = [], jnp.zeros((8, L), F32), jnp.zeros((8, L), F32)
        for h in range(H):
            c_col = jnp.sum(jnp.where(diag, cc[h:h + 1, :], 0.0), axis=1, keepdims=True)
            c_msk = jnp.where(causal, c_col, -jnp.inf)
            c_top = jnp.max(c_msk, axis=0, keepdims=True)
            cmax = jnp.where(row8 == h, c_top, cmax)
            dt = jnp.exp(c_msk - c_top)
            sc = (st_all[h * L:(h + 1) * L, :] * dt).astype(BF16)
            lhs = jnp.concatenate([v_t[h * D:(h + 1) * D, :], ones8], axis=0)
            out = _dot(lhs, sc)
            r_t.append(out[0:D, :])
            den0 = jnp.where(row8 == h, out[D:D + 1, :], den0)
        v_f = v_t.astype(F32)
        vw = jnp.concatenate([v_f[h * D:(h + 1) * D, :] * wk[h:h + 1, :] for h in range(H)] + [wk], axis=0)
        dcn = _dot(vw.astype(BF16), k_n)
        dc = jnp.where(bd_mask, dcn[0:W, :], 0.0)
        dn = jnp.where(n_mask, dcn[W:W + 8, :], 0.0)

        qcn = _dot(jnp.concatenate([cmat, n8], axis=0).astype(BF16), q_t)
        qn = qcn[W:W + 8, :]
        mm = jnp.maximum(m8, cmax)
        a = jnp.exp(cmax - mm)
        b = jnp.exp(m8 - mm)
        den = a * den0 + b * qn
        inv = 1.0 / jnp.maximum(jnp.abs(den), jnp.exp(-bcum - mm))
        ai = a * inv
        bi = b * inv
        m_new = jnp.maximum(blast + m8, gmax)
        decay = jnp.exp(blast + m8 - m_new)
        grow = jnp.exp(gmax - m_new)
        cmat = head_rows(decay) * cmat + head_rows(grow) * dc
        n8 = wide(decay) * n8 + wide(grow) * dn
        m8 = m_new

        y_t = []
        for h in range(H):
            num = ai[h:h + 1, :] * r_t[h] + bi[h:h + 1, :] * qcn[h * D:(h + 1) * D, :]
            dlt = num - jnp.mean(num, axis=0, keepdims=True)
            y_t.append(dlt * lax.rsqrt(jnp.mean(dlt * dlt, axis=0, keepdims=True) + EPS))
        y = jnp.transpose(jnp.concatenate(y_t, axis=0) * ng_ref[...])
        y_ref[c * L:(c + 1) * L, :] = (_sigmoid(o_ref[c * L:(c + 1) * L, :]) * y).astype(y_ref.dtype)

    c_ref[...] = cmat
    n_ref[...] = n8
    m_ref[...] = m8


def _mlstm(mq_t, mk, mv_t, mo, mg_t, norm_g, *, batch, tb=512):
    w, t = mq_t.shape
    s = t // batch
    tb = min(tb, s)
    nb = s // tb

    def cols(rows):
        return pl.BlockSpec((rows, tb), lambda b, i: (0, b * nb + i))

    row_spec = pl.BlockSpec((tb, w), lambda b, i: (b * nb + i, 0))
    return pl.pallas_call(
        functools.partial(_mlstm_kernel, tb=tb),
        out_shape=jax.ShapeDtypeStruct((t, w), BF16),
        grid=(batch, nb),
        in_specs=[cols(w), row_spec, cols(w), row_spec, cols(mg_t.shape[0]), _resident((w, LANES))],
        out_specs=row_spec,
        scratch_shapes=[pltpu.VMEM((w, w), F32), pltpu.VMEM((8, w), F32), pltpu.VMEM((8, LANES), F32)],
        compiler_params=_params("arbitrary", "arbitrary"),
        name="mlstm",
    )(mq_t, mk, mv_t, mo, mg_t, _on_lanes(norm_g))


def _attn_kernel(ti_ref, tj_ref, q_ref, k_ref, v_ref, lam_ref, ng_ref, o_ref, qs_ref, va_ref, m_ref, acc_ref,
                 s0_ref, s1_ref, *, tq, rc, n_off, n_diag, lambda_init):
    nq = q_ref.shape[0] // tq
    comp_a = (lax.broadcasted_iota(jnp.int32, (tq, LANES), 1) % HEAD_DIM) < HEAD_DIM // 2

    def stack_q(i, carry):
        q = q_ref[pl.ds(pl.multiple_of(i * tq, tq), tq), :]
        zero = jnp.zeros_like(q)
        base = pl.multiple_of(i * (2 * tq), 2 * tq)
        qs_ref[pl.ds(base, tq), :] = jnp.where(comp_a, q, zero)
        qs_ref[pl.ds(base + tq, tq), :] = jnp.where(comp_a, zero, q)
        return carry

    lax.fori_loop(0, nq, stack_q, 0)
    va_ref[:, 0:LANES] = v_ref[...]
    va_ref[:, LANES:] = jnp.ones((va_ref.shape[0], LANES), va_ref.dtype)
    m_ref[...] = jnp.full_like(m_ref, -jnp.inf)
    acc_ref[...] = jnp.zeros_like(acc_ref)

    def scores(buf_ref, t):
        q_row = pl.multiple_of(ti_ref[t] * (2 * tq), 2 * tq)
        k_row = pl.multiple_of(tj_ref[t] * tq, tq)
        buf_ref[...] = _dot_nt(qs_ref[pl.ds(q_row, 2 * tq), :], k_ref[pl.ds(k_row, tq), :])

    def update(buf_ref, t, diagonal):
        q_row = ti_ref[t] * (2 * tq)
        k_row = pl.multiple_of(tj_ref[t] * tq, tq)
        for r in range(2 * tq // rc):
            rows = pl.ds(pl.multiple_of(q_row + r * rc, rc), rc)
            first = (r * rc) % tq
            nk = first + rc if diagonal else tq
            s = buf_ref[r * rc:(r + 1) * rc, 0:nk]
            if diagonal:
                row = first + lax.broadcasted_iota(jnp.int32, s.shape, 0)
                col = lax.broadcasted_iota(jnp.int32, s.shape, 1)
                s = jnp.where(col <= row, s, -jnp.inf)
            m_prev = m_ref[rows, :]
            m_next = jnp.maximum(m_prev, jnp.max(s, axis=-1, keepdims=True))
            alpha = jnp.exp2(m_prev - m_next)
            p = jnp.exp2(s - pltpu.repeat(m_next, nk // LANES, 1))
            pv = _dot(p.astype(BF16), va_ref[pl.ds(k_row, nk), :])
            acc_ref[rows, :] = pltpu.repeat(alpha, 2, 1) * acc_ref[rows, :] + pv
            m_ref[rows, :] = m_next

    bufs = (s0_ref, s1_ref)

    def run(t0, count, diagonal, cur):
        a_ref, b_ref = bufs[cur], bufs[1 - cur]

        def pair(pp, carry):
            t = t0 + 2 * pp
            scores(b_ref, t + 1)
            update(a_ref, t, diagonal)
            scores(a_ref, t + 2)
            update(b_ref, t + 1, diagonal)
            return carry

        lax.fori_loop(0, count // 2, pair, 0)
        if count % 2:
            t = t0 + count - 1
            scores(b_ref, t + 1)
            update(a_ref, t, diagonal)
            cur = 1 - cur
        return cur

    scores(s0_ref, 0)
    cur = run(0, n_off, False, 0)
    run(n_off, n_diag, True, cur)

    lam = lam_ref[...]
    lam_full = (jnp.exp(jnp.sum(lam[0:1, :] * lam[1:2, :], axis=-1, keepdims=True))
                - jnp.exp(jnp.sum(lam[2:3, :] * lam[3:4, :], axis=-1, keepdims=True)) + lambda_init)

    def finish(i, carry):
        base = pl.multiple_of(i * (2 * tq), 2 * tq)
        acc_a = acc_ref[pl.ds(base, tq), :]
        acc_b = acc_ref[pl.ds(base + tq, tq), :]
        out = acc_a[:, 0:LANES] / acc_a[:, LANES:] - lam_full * (acc_b[:, 0:LANES] / acc_b[:, LANES:])
        out = out * lax.rsqrt(jnp.mean(out * out, axis=-1, keepdims=True) + EPS) * ng_ref[...] * (1.0 - lambda_init)
        o_ref[pl.ds(pl.multiple_of(i * tq, tq), tq), :] = out.astype(o_ref.dtype)
        return carry

    lax.fori_loop(0, nq, finish, 0)


def _attn(dq, dk, dv, lam, norm_g, lambda_init, *, batch, tq=512, rc=256):
    t, w = dq.shape
    s = t // batch
    tq = min(tq, s)
    nq = s // tq
    below = [(i, j) for j in range(nq) for i in range(j + 1, nq)]
    pairs = below + [(i, i) for i in range(nq)] + [(0, 0)]
    tab_i = jnp.asarray([p[0] for p in pairs], jnp.int32)
    tab_j = jnp.asarray([p[1] for p in pairs], jnp.int32)
    head = pl.BlockSpec((s, LANES), lambda b, h, ti, tj: (b, h))
    grid_spec = pltpu.PrefetchScalarGridSpec(
        num_scalar_prefetch=2,
        grid=(batch, N_HEADS),
        in_specs=[head, head, head,
                  pl.BlockSpec(lam.shape, lambda b, h, ti, tj: (0, 0)),
                  pl.BlockSpec((1, LANES), lambda b, h, ti, tj: (0, 0))],
        out_specs=head,
        scratch_shapes=[pltpu.VMEM((2 * s, LANES), BF16), pltpu.VMEM((s, 2 * LANES), BF16),
                        pltpu.VMEM((2 * s, LANES), F32), pltpu.VMEM((2 * s, 2 * LANES), F32),
                        pltpu.VMEM((2 * tq, tq), F32), pltpu.VMEM((2 * tq, tq), F32)],
    )
    return pl.pallas_call(
        functools.partial(_attn_kernel, tq=tq, rc=rc, n_off=len(below), n_diag=nq, lambda_init=lambda_init),
        out_shape=jax.ShapeDtypeStruct((t, w), BF16),
        grid_spec=grid_spec,
        compiler_params=_params("arbitrary", "arbitrary"),
        name="diff_attn",
    )(tab_i, tab_j, dq, dk, dv, lam, norm_g.reshape(1, LANES))


def _merge_kernel(x_ref, pu_ref, halo_ref, yb_ref, yc_ref, gate_ref, pbd_ref, ps_ref, pa_ref, pb_ref, pc_ref,
                  wo_ref, o_ref, *, tm, seq):
    d = x_ref.shape[1]
    t0 = (pl.program_id(0) * tm) % seq
    u = pu_ref[...]
    halo = jnp.where(t0 > 0, halo_ref[...], 0.0)
    e = jnp.concatenate([halo, u], axis=0)
    lane_grp = lax.broadcasted_iota(jnp.int32, u.shape, 1) // (u.shape[1] // len(POOL_WINDOWS))
    win = None
    shift = 1
    for g in range(len(POOL_WINDOWS)):
        e = e + pltpu.roll(e, shift, 0)
        shift *= 2
        cur = e[POOL_HALO:, :]
        win = cur if win is None else jnp.where(lane_grp >= g, cur, win)
    pos1 = t0 + 1 + lax.broadcasted_iota(jnp.int32, u.shape, 0)
    cnt = jnp.minimum(pos1, jnp.left_shift(2, lane_grp)).astype(F32)
    pooled = win / cnt - u
    ya = _dot(pooled.astype(BF16), pbd_ref[...]) * ps_ref[...]

    def gate(j):
        return _sigmoid(gate_ref[:, j * d:(j + 1) * d].astype(F32))

    merged = gate(0) * _dot(ya.astype(BF16), pa_ref[...])
    merged = merged + gate(1) * _dot(yb_ref[...], pb_ref[...])
    merged = merged + gate(2) * _dot(yc_ref[...], pc_ref[...])
    o_ref[...] = x_ref[...] + _dot(merged.astype(BF16), wo_ref[...])


def _merge(x, pool_u, yb, yc, gate, pool_bd, pool_scale, p_a, p_b, p_c, w_out, *, seq, tm=512):
    t, d = x.shape
    tm = min(tm, seq)
    cp = pool_u.shape[1]
    hb = tm // POOL_HALO

    def rows(width):
        return pl.BlockSpec((tm, width), lambda i: (i, 0))

    return pl.pallas_call(
        functools.partial(_merge_kernel, tm=tm, seq=seq),
        out_shape=jax.ShapeDtypeStruct((t, d), F32),
        grid=(t // tm,),
        in_specs=[rows(d), rows(cp), pl.BlockSpec((POOL_HALO, cp), lambda i: (jnp.maximum(i * hb - 1, 0), 0)),
                  rows(yb.shape[1]), rows(yc.shape[1]), rows(gate.shape[1]),
                  _resident(pool_bd.shape), _resident((1, cp)), _resident(p_a.shape), _resident(p_b.shape),
                  _resident(p_c.shape), _resident(w_out.shape)],
        out_specs=rows(d),
        compiler_params=_params("arbitrary"),
        name="merge",
    )(x, pool_u, pool_u, yb, yc, gate, pool_bd, pool_scale.reshape(1, cp), p_a, p_b, p_c, w_out)


def _rope_perm():
    half = HEAD_DIM // 2
    idx = []
    for h in range(N_HEADS):
        base = h * LANES
        for blk in (0, 2, 1, 3):
            idx.extend(range(base + blk * half, base + (blk + 1) * half))
    return np.asarray(idx)


def _prep_layer(w_in, pool_w):
    d = w_in.shape[0]
    n_pool = pool_w.shape[0] * pool_w.shape[1]
    mw = N_HEADS * HEAD_DIM
    sizes = (n_pool, mw, mw, mw, mw, 2 * N_HEADS, 2 * mw, 2 * mw, 2 * mw, 3 * d)
    offs = np.concatenate([[0], np.cumsum(sizes)])
    u_pool, m_q, m_k, m_v, m_o, m_if, d_q, d_k, d_v, gate = (w_in[:, offs[j]:offs[j + 1]] for j in range(len(sizes)))
    perm = _rope_perm()
    w_big = jnp.concatenate([m_o, u_pool, d_q[:, perm], d_k[:, perm], d_v, gate], axis=1).astype(BF16)
    w_t = jnp.concatenate([m_if, m_q, m_k, m_v], axis=1).T.astype(BF16)
    pool_bd = jax.scipy.linalg.block_diag(*[pool_w[g] for g in range(pool_w.shape[0])]).astype(BF16)
    return w_big, w_t, pool_bd


def kernel(x, positions, ffn1_norm, ffn1_w13, ffn1_w2, mix_norm, w_in, pool_w, pool_scale, m_conv_w, m_conv_b,
           m_gate_b, m_norm, d_lambda, d_norm, p_a, p_b, p_c, w_out, ffn2_norm, ffn2_w13, ffn2_w2, final_norm):
    batch, seq, d = x.shape
    depth = w_in.shape[0]
    cosf, sinf = _rope_tables(positions)
    h = x.reshape(batch * seq, d)
    ffn1_w13, ffn1_w2, ffn2_w13, ffn2_w2 = (w.astype(BF16) for w in (ffn1_w13, ffn1_w2, ffn2_w13, ffn2_w2))
    for l in range(depth):
        lambda_init = 0.8 - 0.6 * math.exp(-0.3 * l)
        w_big, w_t, pool_bd = _prep_layer(w_in[l], pool_w[l])
        h = _ffn(h, ffn1_norm[l], ffn1_w13, ffn1_w2, l)
        mq, mk, mv, mo, pool_u, mg, dq, dk, dv, gate = _inproj(h, mix_norm[l], w_big, w_t, m_conv_w[l], m_conv_b[l],
                                                              m_gate_b[l], cosf, sinf, seq=seq)
        yb = _mlstm(mq, mk, mv, mo, mg, m_norm[l], batch=batch)
        yc = _attn(dq, dk, dv, d_lambda[l], d_norm[l], lambda_init, batch=batch)
        h = _merge(h, pool_u, yb, yc, gate, pool_bd, pool_scale[l], p_a[l].astype(BF16), p_b[l].astype(BF16),
                   p_c[l].astype(BF16), w_out[l].astype(BF16), seq=seq)
        h = _ffn(h, ffn2_norm[l], ffn2_w13, ffn2_w2, l, final_norm if l == depth - 1 else None)
    return h.reshape(batch, seq, d)
```

```python
import functools
import math

import jax
import jax.numpy as jnp
import numpy as np
from jax import lax
from jax.experimental import pallas as pl
from jax.experimental.pallas import tpu as pltpu

F32 = jnp.float32
BF16 = jnp.bfloat16
EPS = 1e-6
ROPE_THETA = 10000.0
POOL_WINDOWS = (2, 4, 8, 16)
N_HEADS = 4
HEAD_DIM = 64
LANES = 128
MLSTM_CHUNK = 128
LOG2E = math.log2(math.e)
POOL_HALO = 16
VMEM_LIMIT = 56 * 1024 * 1024


def _params(*sem):
    return pltpu.CompilerParams(dimension_semantics=sem, vmem_limit_bytes=VMEM_LIMIT)


def _resident(shape):
    zeros = (0,) * len(shape)
    return pl.BlockSpec(shape, lambda *_: zeros, pipeline_mode=pl.Buffered(1))


def _on_lanes(a):
    return jnp.broadcast_to(a[..., None], a.shape + (LANES,))


def _lane_scan(x, lane, combine, fill):
    sh = 1
    while sh < x.shape[1]:
        x = combine(x, jnp.where(lane >= sh, pltpu.roll(x, sh, 1), fill))
        sh *= 2
    return x


def _rms(x, g):
    return x * lax.rsqrt(jnp.mean(x * x, axis=-1, keepdims=True) + EPS) * g


def _sigmoid(x):
    return 1.0 / (1.0 + jnp.exp(-x))


def _dot(a, b):
    return jnp.dot(a, b, preferred_element_type=F32)


def _dot_nt(a, b):
    return lax.dot_general(a, b, (((1,), (1,)), ((), ())), preferred_element_type=F32)


def _rope_kernel(pos_ref, inv_ref, cos_ref, sin_ref):
    tr = pos_ref.shape[1]
    for c in range(tr // LANES):
        sl = slice(c * LANES, (c + 1) * LANES)
        ang = pos_ref[:, sl].astype(F32) * inv_ref[...]
        cos_ref[:, sl] = jnp.cos(ang)
        sin_ref[:, sl] = jnp.sin(ang)


def _rope_tables(positions):
    t = positions.size
    half = HEAD_DIM // 2
    inv = 1.0 / (ROPE_THETA ** (jnp.arange(0, HEAD_DIM, 2, dtype=F32) / HEAD_DIM))
    tr = min(t, 4096)
    spec = pl.BlockSpec((half, tr), lambda i: (0, i))
    return pl.pallas_call(
        _rope_kernel,
        out_shape=[jax.ShapeDtypeStruct((half, t), F32)] * 2,
        grid=(t // tr,),
        in_specs=[pl.BlockSpec((1, tr), lambda i: (0, i)), pl.BlockSpec((half, LANES), lambda i: (0, 0))],
        out_specs=[spec] * 2,
        compiler_params=_params("arbitrary"),
        name="rope_tables",
    )(positions.reshape(1, t), _on_lanes(inv))


def _swiglu_half_step(x, g_ref, w13_ref, w2_ref, acc_ref, tf):
    d_ff = w2_ref.shape[0]
    xn = _rms(x, g_ref[...]).astype(BF16)
    for c in range(d_ff // tf):
        lo, hi = c * tf, (c + 1) * tf
        gate = _dot(xn, w13_ref[:, lo:hi])
        up = _dot(xn, w13_ref[:, d_ff + lo:d_ff + hi])
        act = (gate * _sigmoid(gate) * up).astype(BF16)
        part = _dot(act, w2_ref[lo:hi, :])
        if c == 0:
            acc_ref[...] = part
        else:
            acc_ref[...] += part
    return x + 0.5 * acc_ref[...]


def _ffn_kernel(x_ref, g_ref, w13_ref, w2_ref, o_ref, acc_ref, *, tf):
    o_ref[...] = _swiglu_half_step(x_ref[...], g_ref, w13_ref, w2_ref, acc_ref, tf)


def _layer_weight(shape, layer):
    return pl.BlockSpec((None,) + shape, lambda i: (layer, 0, 0), pipeline_mode=pl.Buffered(1))


def _ffn(x, g, w13, w2, layer, *, tm=512, tf=256):
    t, d = x.shape
    d_ff = w2.shape[1]
    tm = min(tm, t)
    row = pl.BlockSpec((tm, d), lambda i: (i, 0))
    return pl.pallas_call(
        functools.partial(_ffn_kernel, tf=tf),
        out_shape=jax.ShapeDtypeStruct((t, d), F32),
        grid=(t // tm,),
        in_specs=[row, _resident((1, d)), _layer_weight((d, 2 * d_ff), layer), _layer_weight((d_ff, d), layer)],
        out_specs=row,
        scratch_shapes=[pltpu.VMEM((tm, d), F32)],
        compiler_params=_params("arbitrary"),
        name="ffn",
    )(x, g.reshape(1, d), w13, w2)


_C_MO, _C_POOL, _C_DV, _C_GATE, _C_END = 0, 256, 512, 1024, 4096
_R_IF, _R_QK, _R_V, _R_DQ, _R_DK, _R_END = 0, 8, 520, 776, 1288, 1800


def _inproj_kernel(x_ref, g_ref, w_ref, wt_ref, cw_ref, cb_ref, gb_ref, cos_ref, sin_ref,
                   mq_ref, mk_ref, mv_ref, mo_ref, pool_ref, mg_ref, dq_ref, dk_ref, dv_ref, gate_ref,
                   ext_ref, *, blocks_per_seq):
    tm = x_ref.shape[0]
    halo = ext_ref.shape[1] - tm
    kconv = cw_ref.shape[0]
    mw = mq_ref.shape[0]
    xn = _rms(x_ref[...], g_ref[...]).astype(BF16)

    def proj(lo, hi):
        return _dot(xn, w_ref[:, lo:hi])

    zt = _dot_nt(wt_ref[...], xn)
    mv_ref[...] = zt[_R_V:_R_DQ, :].astype(BF16)

    lane8 = lax.broadcasted_iota(jnp.int32, (8, MLSTM_CHUNK), 1)
    for c in range(tm // MLSTM_CHUNK):
        sl = slice(c * MLSTM_CHUNK, (c + 1) * MLSTM_CHUNK)
        g8 = zt[_R_IF:_R_QK, sl] + gb_ref[...]
        logf = pltpu.roll(jnp.minimum(g8, 0.0) - jnp.log1p(jnp.exp(-jnp.abs(g8))), N_HEADS, 0)
        bcum = _lane_scan(logf, lane8, jnp.add, 0.0)
        cc = g8 - bcum
        mg_ref[0:8, sl] = bcum
        mg_ref[8:16, sl] = cc
        mg_ref[16:24, sl] = jnp.broadcast_to(jnp.sum(logf, axis=1, keepdims=True), cc.shape)
        mg_ref[24:32, sl] = jnp.broadcast_to(jnp.max(cc, axis=1, keepdims=True), cc.shape)

    @pl.when(pl.program_id(0) % blocks_per_seq == 0)
    def _():
        ext_ref[:, 0:halo] = jnp.zeros((ext_ref.shape[0], halo), F32)

    ext_ref[:, halo:halo + tm] = zt[_R_QK:_R_V, :]
    ext = ext_ref[...]
    taps = [ext if j == kconv - 1 else pltpu.roll(ext, kconv - 1 - j, 1) for j in range(kconv)]
    ext_ref[:, 0:halo] = ext_ref[:, tm:tm + halo]
    for c in range(tm // LANES):
        conv = cb_ref[...]
        for j in range(kconv):
            conv = conv + cw_ref[j] * taps[j][:, halo + c * LANES:halo + (c + 1) * LANES]
        qk = conv * _sigmoid(conv)
        mq_ref[:, c * LANES:(c + 1) * LANES] = qk[0:mw, :].astype(BF16)
        mk_ref[c * LANES:(c + 1) * LANES, :] = jnp.transpose(qk[mw:2 * mw, :] * (HEAD_DIM ** -0.5)).astype(BF16)
    mo_ref[...] = proj(_C_MO, _C_POOL)
    pool_ref[...] = proj(_C_POOL, _C_DV)

    half = HEAD_DIM // 2
    cos = cos_ref[...]
    sin = sin_ref[...]
    for ref, lo, scale in ((dq_ref, _R_DQ, HEAD_DIM ** -0.5 * LOG2E), (dk_ref, _R_DK, 1.0)):
        for h in range(N_HEADS):
            parts = []
            for comp in range(2):
                base = lo + h * LANES + comp * HEAD_DIM
                x1 = zt[base:base + half, :]
                x2 = zt[base + half:base + HEAD_DIM, :]
                parts += [x1 * cos - x2 * sin, x2 * cos + x1 * sin]
            head = jnp.concatenate(parts, axis=0) * scale
            for c in range(tm // LANES):
                ref[c * LANES:(c + 1) * LANES, h * LANES:(h + 1) * LANES] = jnp.transpose(
                    head[:, c * LANES:(c + 1) * LANES]).astype(BF16)
    dv_ref[...] = proj(_C_DV, _C_GATE).astype(BF16)
    gate_ref[...] = proj(_C_GATE, _C_END).astype(BF16)


def _inproj(x, g, w_big, w_t, conv_w, conv_b, gate_b, cosf, sinf, *, seq, tm=512):
    t, d = x.shape
    tm = min(tm, seq)
    mw = (_R_V - _R_QK) // 2
    kconv = conv_w.shape[0]

    def rows(width):
        return pl.BlockSpec((tm, width), lambda i: (i, 0))

    def cols(height):
        return pl.BlockSpec((height, tm), lambda i: (0, i))

    out_shape = [jax.ShapeDtypeStruct((mw, t), BF16), jax.ShapeDtypeStruct((t, mw), BF16),
                 jax.ShapeDtypeStruct((mw, t), BF16)]
    out_specs = [cols(mw), rows(mw), cols(mw)]
    for width in (_C_POOL - _C_MO, _C_DV - _C_POOL):
        out_shape.append(jax.ShapeDtypeStruct((t, width), F32))
        out_specs.append(rows(width))
    out_shape.append(jax.ShapeDtypeStruct((4 * (_R_QK - _R_IF), t), F32))
    out_specs.append(cols(4 * (_R_QK - _R_IF)))
    for w in (512, 512, 512, 3072):
        out_shape.append(jax.ShapeDtypeStruct((t, w), BF16))
        out_specs.append(rows(w))
    return pl.pallas_call(
        functools.partial(_inproj_kernel, blocks_per_seq=seq // tm),
        out_shape=out_shape,
        grid=(t // tm,),
        in_specs=[rows(d), _resident((1, d)), _resident(w_big.shape), _resident(w_t.shape),
                  _resident((kconv, 2 * mw, LANES)), _resident((2 * mw, LANES)), _resident((8, LANES)),
                  cols(HEAD_DIM // 2), cols(HEAD_DIM // 2)],
        out_specs=out_specs,
        scratch_shapes=[pltpu.VMEM((2 * mw, tm + LANES), F32)],
        compiler_params=_params("arbitrary"),
        name="inproj",
    )(x, g.reshape(1, d), w_big, w_t, _on_lanes(conv_w), _on_lanes(conv_b), _on_lanes(gate_b), cosf, sinf)


def _head_of_lane(shape, dim):
    return lax.broadcasted_iota(jnp.int32, shape, dim) // HEAD_DIM


def _mlstm_kernel(q_ref, k_ref, v_ref, o_ref, g_ref, ng_ref, y_ref, c_ref, n_ref, m_ref, *, tb):
    L = MLSTM_CHUNK
    H, D = N_HEADS, HEAD_DIM
    W = H * D

    @pl.when(pl.program_id(1) == 0)
    def _():
        c_ref[...] = jnp.zeros_like(c_ref)
        n_ref[...] = jnp.zeros_like(n_ref)
        m_ref[...] = jnp.zeros_like(m_ref)

    row8 = lax.broadcasted_iota(jnp.int32, (8, L), 0)
    ss = lax.broadcasted_iota(jnp.int32, (L, L), 0)
    tt = lax.broadcasted_iota(jnp.int32, (L, L), 1)
    causal = ss <= tt
    diag = ss == tt
    lane_head = _head_of_lane((L, W), 1)
    bd_mask = _head_of_lane((W, W), 0) == _head_of_lane((W, W), 1)
    n_mask = lax.broadcasted_iota(jnp.int32, (8, W), 0) == _head_of_lane((8, W), 1)
    ones8 = jnp.ones((8, L), BF16)

    def head_rows(x8):
        col = jnp.concatenate([jnp.broadcast_to(x8[h:h + 1, :], (D, L)) for h in range(H)], axis=0)
        return jnp.concatenate([col] * (W // L), axis=1)

    def wide(x8):
        return jnp.concatenate([x8] * (W // L), axis=1)

    cmat = c_ref[...]
    n8 = n_ref[...]
    m8 = m_ref[...]

    for c in range(tb // L):
        q_t = q_ref[:, c * L:(c + 1) * L]
        k_n = k_ref[c * L:(c + 1) * L, :]
        v_t = v_ref[:, c * L:(c + 1) * L]

        bcum = g_ref[0:8, c * L:(c + 1) * L]
        cc = g_ref[8:16, c * L:(c + 1) * L]
        blast = g_ref[16:24, c * L:(c + 1) * L]
        call = g_ref[24:32, c * L:(c + 1) * L]
        gmax = blast + call
        wk = jnp.exp(cc - call)

        k_stack = jnp.concatenate([jnp.where(lane_head == h, k_n, jnp.zeros_like(k_n)) for h in range(H)], axis=0)
        st_all = _dot(k_stack, q_t)
        r_t, den0, cmax = [], jnp.zeros((8, L), F32), jnp.zeros((8, L), F32)
        for h in range(H):
            c_col = jnp.sum(jnp.where(diag, cc[h:h + 1, :], 0.0), axis=1, keepdims=True)
            c_msk = jnp.where(causal, c_col, -jnp.inf)
            c_top = jnp.max(c_msk, axis=0, keepdims=True)
            cmax = jnp.where(row8 == h, c_top, cmax)
            dt = jnp.exp(c_msk - c_top)
            sc = (st_all[h * L:(h + 1) * L, :] * dt).astype(BF16)
            lhs = jnp.concatenate([v_t[h * D:(h + 1) * D, :], ones8], axis=0)
            out = _dot(lhs, sc)
            r_t.append(out[0:D, :])
            den0 = jnp.where(row8 == h, out[D:D + 1, :], den0)
        v_f = v_t.astype(F32)
        vw = jnp.concatenate([v_f[h * D:(h + 1) * D, :] * wk[h:h + 1, :] for h in range(H)] + [wk], axis=0)
        dcn = _dot(vw.astype(BF16), k_n)
        dc = jnp.where(bd_mask, dcn[0:W, :], 0.0)
        dn = jnp.where(n_mask, dcn[W:W + 8, :], 0.0)

        qcn = _dot(jnp.concatenate([cmat, n8], axis=0).astype(BF16), q_t)
        qn = qcn[W:W + 8, :]
        mm = jnp.maximum(m8, cmax)
        a = jnp.exp(cmax - mm)
        b = jnp.exp(m8 - mm)
        den = a * den0 + b * qn
        inv = 1.0 / jnp.maximum(jnp.abs(den), jnp.exp(-bcum - mm))
        ai = a * inv
        bi = b * inv
        m_new = jnp.maximum(blast + m8, gmax)
        decay = jnp.exp(blast + m8 - m_new)
        grow = jnp.exp(gmax - m_new)
        cmat = head_rows(decay) * cmat + head_rows(grow) * dc
        n8 = wide(decay) * n8 + wide(grow) * dn
        m8 = m_new

        y_t = []
        for h in range(H):
            num = ai[h:h + 1, :] * r_t[h] + bi[h:h + 1, :] * qcn[h * D:(h + 1) * D, :]
            dlt = num - jnp.mean(num, axis=0, keepdims=True)
            y_t.append(dlt * lax.rsqrt(jnp.mean(dlt * dlt, axis=0, keepdims=True) + EPS))
        y = jnp.transpose(jnp.concatenate(y_t, axis=0) * ng_ref[...])
        y_ref[c * L:(c + 1) * L, :] = (_sigmoid(o_ref[c * L:(c + 1) * L, :]) * y).astype(y_ref.dtype)

    c_ref[...] = cmat
    n_ref[...] = n8
    m_ref[...] = m8


def _mlstm(mq_t, mk, mv_t, mo, mg_t, norm_g, *, batch, tb=512):
    w, t = mq_t.shape
    s = t // batch
    tb = min(tb, s)
    nb = s // tb

    def cols(rows):
        return pl.BlockSpec((rows, tb), lambda b, i: (0, b * nb + i))

    row_spec = pl.BlockSpec((tb, w), lambda b, i: (b * nb + i, 0))
    return pl.pallas_call(
        functools.partial(_mlstm_kernel, tb=tb),
        out_shape=jax.ShapeDtypeStruct((t, w), BF16),
        grid=(batch, nb),
        in_specs=[cols(w), row_spec, cols(w), row_spec, cols(mg_t.shape[0]), _resident((w, LANES))],
        out_specs=row_spec,
        scratch_shapes=[pltpu.VMEM((w, w), F32), pltpu.VMEM((8, w), F32), pltpu.VMEM((8, LANES), F32)],
        compiler_params=_params("arbitrary", "arbitrary"),
        name="mlstm",
    )(mq_t, mk, mv_t, mo, mg_t, _on_lanes(norm_g))


def _attn_kernel(ti_ref, tj_ref, q_ref, k_ref, v_ref, lam_ref, ng_ref, o_ref, qs_ref, va_ref, m_ref, acc_ref,
                 s0_ref, s1_ref, *, tq, rc, n_off, n_diag, lambda_init):
    nq = q_ref.shape[0] // tq
    comp_a = lax.broadcasted_iota(jnp.int32, (tq, LANES), 1) < HEAD_DIM

    def stack_q(i, carry):
        q = q_ref[pl.ds(pl.multiple_of(i * tq, tq), tq), :]
        zero = jnp.zeros_like(q)
        base = pl.multiple_of(i * (2 * tq), 2 * tq)
        qs_ref[pl.ds(base, tq), :] = jnp.where(comp_a, q, zero)
        qs_ref[pl.ds(base + tq, tq), :] = jnp.where(comp_a, zero, q)
        return carry

    lax.fori_loop(0, nq, stack_q, 0)
    va_ref[:, 0:LANES] = v_ref[...]
    va_ref[:, LANES:] = jnp.ones((va_ref.shape[0], LANES), va_ref.dtype)
    m_ref[...] = jnp.full_like(m_ref, -jnp.inf)
    acc_ref[...] = jnp.zeros_like(acc_ref)

    def scores(buf_ref, t):
        q_row = pl.multiple_of(ti_ref[t] * (2 * tq), 2 * tq)
        k_row = pl.multiple_of(tj_ref[t] * tq, tq)
        buf_ref[...] = _dot_nt(qs_ref[pl.ds(q_row, 2 * tq), :], k_ref[pl.ds(k_row, tq), :])

    def update(buf_ref, t, diagonal):
        q_row = ti_ref[t] * (2 * tq)
        k_row = pl.multiple_of(tj_ref[t] * tq, tq)
        for r in range(2 * tq // rc):
            rows = pl.ds(pl.multiple_of(q_row + r * rc, rc), rc)
            first = (r * rc) % tq
            nk = first + rc if diagonal else tq
            s = buf_ref[r * rc:(r + 1) * rc, 0:nk]
            if diagonal:
                row = first + lax.broadcasted_iota(jnp.int32, s.shape, 0)
                col = lax.broadcasted_iota(jnp.int32, s.shape, 1)
                s = jnp.where(col <= row, s, -jnp.inf)
            m_prev = m_ref[rows, :]
            m_next = jnp.maximum(m_prev, jnp.max(s, axis=-1, keepdims=True))
            alpha = jnp.exp2(m_prev - m_next)
            p = jnp.exp2(s - jnp.concatenate([m_next] * (nk // LANES), axis=1))
            pv = _dot(p.astype(BF16), va_ref[pl.ds(k_row, nk), :])
            acc_ref[rows, :] = jnp.concatenate([alpha, alpha], axis=1) * acc_ref[rows, :] + pv
            m_ref[rows, :] = m_next

    bufs = (s0_ref, s1_ref)

    def run(t0, count, diagonal, cur):
        a_ref, b_ref = bufs[cur], bufs[1 - cur]

        def pair(pp, carry):
            t = t0 + 2 * pp
            scores(b_ref, t + 1)
            update(a_ref, t, diagonal)
            scores(a_ref, t + 2)
            update(b_ref, t + 1, diagonal)
            return carry

        lax.fori_loop(0, count // 2, pair, 0)
        if count % 2:
            t = t0 + count - 1
            scores(b_ref, t + 1)
            update(a_ref, t, diagonal)
            cur = 1 - cur
        return cur

    scores(s0_ref, 0)
    cur = run(0, n_off, False, 0)
    run(n_off, n_diag, True, cur)

    lam = lam_ref[...]
    lam_full = (jnp.exp(jnp.sum(lam[0:1, :] * lam[1:2, :], axis=-1, keepdims=True))
                - jnp.exp(jnp.sum(lam[2:3, :] * lam[3:4, :], axis=-1, keepdims=True)) + lambda_init)

    def finish(i, carry):
        base = pl.multiple_of(i * (2 * tq), 2 * tq)
        acc_a = acc_ref[pl.ds(base, tq), :]
        acc_b = acc_ref[pl.ds(base + tq, tq), :]
        out = acc_a[:, 0:LANES] / acc_a[:, LANES:] - lam_full * (acc_b[:, 0:LANES] / acc_b[:, LANES:])
        out = out * lax.rsqrt(jnp.mean(out * out, axis=-1, keepdims=True) + EPS) * ng_ref[...] * (1.0 - lambda_init)
        o_ref[pl.ds(pl.multiple_of(i * tq, tq), tq), :] = out.astype(o_ref.dtype)
        return carry

    lax.fori_loop(0, nq, finish, 0)


def _attn(dq, dk, dv, lam, norm_g, lambda_init, *, batch, tq=512, rc=256):
    t, w = dq.shape
    s = t // batch
    tq = min(tq, s)
    nq = s // tq
    below = [(i, j) for j in range(nq) for i in range(j + 1, nq)]
    pairs = below + [(i, i) for i in range(nq)] + [(0, 0)]
    tab_i = jnp.asarray([p[0] for p in pairs], jnp.int32)
    tab_j = jnp.asarray([p[1] for p in pairs], jnp.int32)
    head = pl.BlockSpec((s, LANES), lambda b, h, ti, tj: (b, h))
    grid_spec = pltpu.PrefetchScalarGridSpec(
        num_scalar_prefetch=2,
        grid=(batch, N_HEADS),
        in_specs=[head, head, head,
                  pl.BlockSpec(lam.shape, lambda b, h, ti, tj: (0, 0)),
                  pl.BlockSpec((1, LANES), lambda b, h, ti, tj: (0, 0))],
        out_specs=head,
        scratch_shapes=[pltpu.VMEM((2 * s, LANES), BF16), pltpu.VMEM((s, 2 * LANES), BF16),
                        pltpu.VMEM((2 * s, LANES), F32), pltpu.VMEM((2 * s, 2 * LANES), F32),
                        pltpu.VMEM((2 * tq, tq), F32), pltpu.VMEM((2 * tq, tq), F32)],
    )
    return pl.pallas_call(
        functools.partial(_attn_kernel, tq=tq, rc=rc, n_off=len(below), n_diag=nq, lambda_init=lambda_init),
        out_shape=jax.ShapeDtypeStruct((t, w), BF16),
        grid_spec=grid_spec,
        compiler_params=_params("arbitrary", "arbitrary"),
        name="diff_attn",
    )(tab_i, tab_j, dq, dk, dv, lam, norm_g.reshape(1, LANES))


def _merge_kernel(x_ref, pu_ref, halo_ref, yb_ref, yc_ref, gate_ref, pbd_ref, ps_ref, pa_ref, pb_ref, pc_ref,
                  wo_ref, g_ref, w13_ref, w2_ref, *rest, tm, seq, tf, final):
    if final:
        fg_ref, o_ref, acc_ref = rest
    else:
        o_ref, acc_ref = rest
    d = x_ref.shape[1]
    t0 = (pl.program_id(0) * tm) % seq
    u = pu_ref[...]
    halo = jnp.where(t0 > 0, halo_ref[...], 0.0)
    e = jnp.concatenate([halo, u], axis=0)
    lane_grp = lax.broadcasted_iota(jnp.int32, u.shape, 1) // (u.shape[1] // len(POOL_WINDOWS))
    win = None
    shift = 1
    for g in range(len(POOL_WINDOWS)):
        e = e + pltpu.roll(e, shift, 0)
        shift *= 2
        cur = e[POOL_HALO:, :]
        win = cur if win is None else jnp.where(lane_grp >= g, cur, win)
    pos1 = t0 + 1 + lax.broadcasted_iota(jnp.int32, u.shape, 0)
    cnt = jnp.minimum(pos1, jnp.left_shift(2, lane_grp)).astype(F32)
    pooled = win / cnt - u
    ya = _dot(pooled.astype(BF16), pbd_ref[...]) * ps_ref[...]

    def gate(j):
        return _sigmoid(gate_ref[:, j * d:(j + 1) * d].astype(F32))

    merged = gate(0) * _dot(ya.astype(BF16), pa_ref[...])
    merged = merged + gate(1) * _dot(yb_ref[...], pb_ref[...])
    merged = merged + gate(2) * _dot(yc_ref[...], pc_ref[...])
    x = x_ref[...] + _dot(merged.astype(BF16), wo_ref[...])
    y = _swiglu_half_step(x, g_ref, w13_ref, w2_ref, acc_ref, tf)
    if final:
        y = _rms(y, fg_ref[...])
    o_ref[...] = y


def _merge_ffn(x, pool_u, yb, yc, gate, pool_bd, pool_scale, p_a, p_b, p_c, w_out, g, w13, w2, layer, final_g=None,
               *, seq, tm=512, tf=256):
    t, d = x.shape
    tm = min(tm, seq)
    cp = pool_u.shape[1]
    hb = tm // POOL_HALO
    d_ff = w2.shape[1]
    final = final_g is not None

    def rows(width):
        return pl.BlockSpec((tm, width), lambda i: (i, 0))

    in_specs = [rows(d), rows(cp), pl.BlockSpec((POOL_HALO, cp), lambda i: (jnp.maximum(i * hb - 1, 0), 0)),
                rows(yb.shape[1]), rows(yc.shape[1]), rows(gate.shape[1]),
                _resident(pool_bd.shape), _resident((1, cp)), _resident(p_a.shape), _resident(p_b.shape),
                _resident(p_c.shape), _resident(w_out.shape),
                _resident((1, d)), _layer_weight((d, 2 * d_ff), layer), _layer_weight((d_ff, d), layer)]
    args = [x, pool_u, pool_u, yb, yc, gate, pool_bd, pool_scale.reshape(1, cp), p_a, p_b, p_c, w_out,
            g.reshape(1, d), w13, w2]
    if final:
        in_specs.append(_resident((1, d)))
        args.append(final_g.reshape(1, d))
    return pl.pallas_call(
        functools.partial(_merge_kernel, tm=tm, seq=seq, tf=tf, final=final),
        out_shape=jax.ShapeDtypeStruct((t, d), F32),
        grid=(t // tm,),
        in_specs=in_specs,
        out_specs=rows(d),
        scratch_shapes=[pltpu.VMEM((tm, d), F32)],
        compiler_params=_params("arbitrary"),
        name="merge_ffn_final" if final else "merge_ffn",
    )(*args)


def _prep_layer(w_in, pool_w):
    d = w_in.shape[0]
    n_pool = pool_w.shape[0] * pool_w.shape[1]
    mw = N_HEADS * HEAD_DIM
    sizes = (n_pool, mw, mw, mw, mw, 2 * N_HEADS, 2 * mw, 2 * mw, 2 * mw, 3 * d)
    offs = np.concatenate([[0], np.cumsum(sizes)])
    u_pool, m_q, m_k, m_v, m_o, m_if, d_q, d_k, d_v, gate = (w_in[:, offs[j]:offs[j + 1]] for j in range(len(sizes)))
    w_big = jnp.concatenate([m_o, u_pool, d_v, gate], axis=1).astype(BF16)
    w_t = jnp.concatenate([m_if, m_q, m_k, m_v, d_q, d_k], axis=1).T.astype(BF16)
    pool_bd = jax.scipy.linalg.block_diag(*[pool_w[g] for g in range(pool_w.shape[0])]).astype(BF16)
    return w_big, w_t, pool_bd


def kernel(x, positions, ffn1_norm, ffn1_w13, ffn1_w2, mix_norm, w_in, pool_w, pool_scale, m_conv_w, m_conv_b,
           m_gate_b, m_norm, d_lambda, d_norm, p_a, p_b, p_c, w_out, ffn2_norm, ffn2_w13, ffn2_w2, final_norm):
    batch, seq, d = x.shape
    depth = w_in.shape[0]
    cosf, sinf = _rope_tables(positions)
    h = x.reshape(batch * seq, d)
    ffn1_w13, ffn1_w2, ffn2_w13, ffn2_w2 = (w.astype(BF16) for w in (ffn1_w13, ffn1_w2, ffn2_w13, ffn2_w2))
    for l in range(depth):
        lambda_init = 0.8 - 0.6 * math.exp(-0.3 * l)
        w_big, w_t, pool_bd = _prep_layer(w_in[l], pool_w[l])
        h = _ffn(h, ffn1_norm[l], ffn1_w13, ffn1_w2, l)
        mq, mk, mv, mo, pool_u, mg, dq, dk, dv, gate = _inproj(h, mix_norm[l], w_big, w_t, m_conv_w[l], m_conv_b[l],
                                                              m_gate_b[l], cosf, sinf, seq=seq)
        yb = _mlstm(mq, mk, mv, mo, mg, m_norm[l], batch=batch)
        yc = _attn(dq, dk, dv, d_lambda[l], d_norm[l], lambda_init, batch=batch)
        h = _merge_ffn(h, pool_u, yb, yc, gate, pool_bd, pool_scale[l], p_a[l].astype(BF16), p_b[l].astype(BF16),
                       p_c[l].astype(BF16), w_out[l].astype(BF16), ffn2_norm[l], ffn2_w13, ffn2_w2, l,
                       final_norm if l == depth - 1 else None, seq=seq)
    return h.reshape(batch, seq, d)
```

```python
import functools
import math

import jax
import jax.numpy as jnp
import numpy as np
from jax import lax
from jax.experimental import pallas as pl
from jax.experimental.pallas import tpu as pltpu

F32 = jnp.float32
BF16 = jnp.bfloat16
EPS = 1e-6
ROPE_THETA = 10000.0
POOL_WINDOWS = (2, 4, 8, 16)
N_HEADS = 4
HEAD_DIM = 64
LANES = 128
MLSTM_CHUNK = 128
ATTN_UNROLL = 14
LOG2E = math.log2(math.e)
POOL_HALO = 16
VMEM_LIMIT = 56 * 1024 * 1024


def _params(*sem):
    return pltpu.CompilerParams(dimension_semantics=sem, vmem_limit_bytes=VMEM_LIMIT)


def _resident(shape):
    zeros = (0,) * len(shape)
    return pl.BlockSpec(shape, lambda *_: zeros, pipeline_mode=pl.Buffered(1))


def _on_lanes(a):
    return jnp.broadcast_to(a[..., None], a.shape + (LANES,))


def _lane_scan(x, lane, combine, fill):
    sh = 1
    while sh < x.shape[1]:
        x = combine(x, jnp.where(lane >= sh, pltpu.roll(x, sh, 1), fill))
        sh *= 2
    return x


def _rms(x, g):
    return x * lax.rsqrt(jnp.mean(x * x, axis=-1, keepdims=True) + EPS) * g


def _sigmoid(x):
    return 1.0 / (1.0 + jnp.exp(-x))


def _dot(a, b):
    return jnp.dot(a, b, preferred_element_type=F32)


def _dot_nt(a, b):
    return lax.dot_general(a, b, (((1,), (1,)), ((), ())), preferred_element_type=F32)


def _rope_kernel(pos_ref, inv_ref, cos_ref, sin_ref):
    tr = pos_ref.shape[1]
    for c in range(tr // LANES):
        sl = slice(c * LANES, (c + 1) * LANES)
        ang = pos_ref[:, sl].astype(F32) * inv_ref[...]
        cos_ref[:, sl] = jnp.cos(ang)
        sin_ref[:, sl] = jnp.sin(ang)


def _rope_tables(positions):
    t = positions.size
    half = HEAD_DIM // 2
    inv = 1.0 / (ROPE_THETA ** (jnp.arange(0, HEAD_DIM, 2, dtype=F32) / HEAD_DIM))
    tr = min(t, 4096)
    spec = pl.BlockSpec((half, tr), lambda i: (0, i))
    return pl.pallas_call(
        _rope_kernel,
        out_shape=[jax.ShapeDtypeStruct((half, t), F32)] * 2,
        grid=(t // tr,),
        in_specs=[pl.BlockSpec((1, tr), lambda i: (0, i)), pl.BlockSpec((half, LANES), lambda i: (0, 0))],
        out_specs=[spec] * 2,
        compiler_params=_params("arbitrary"),
        name="rope_tables",
    )(positions.reshape(1, t), _on_lanes(inv))


def _swiglu_half_step(x, g_ref, w13_ref, w2_ref, acc_ref, tf):
    d_ff = w2_ref.shape[0]
    xn = _rms(x, g_ref[...]).astype(BF16)
    for c in range(d_ff // tf):
        lo, hi = c * tf, (c + 1) * tf
        gate = _dot(xn, w13_ref[:, lo:hi])
        up = _dot(xn, w13_ref[:, d_ff + lo:d_ff + hi])
        act = (gate * _sigmoid(gate) * up).astype(BF16)
        part = _dot(act, w2_ref[lo:hi, :])
        if c == 0:
            acc_ref[...] = part
        else:
            acc_ref[...] += part
    return x + 0.5 * acc_ref[...]


def _ffn_kernel(x_ref, g_ref, w13_ref, w2_ref, o_ref, acc_ref, *, tf):
    o_ref[...] = _swiglu_half_step(x_ref[...], g_ref, w13_ref, w2_ref, acc_ref, tf)


def _layer_weight(shape, layer):
    return pl.BlockSpec((None,) + shape, lambda i: (layer, 0, 0), pipeline_mode=pl.Buffered(1))


def _ffn(x, g, w13, w2, layer, *, tm=512, tf=256):
    t, d = x.shape
    d_ff = w2.shape[1]
    tm = min(tm, t)
    row = pl.BlockSpec((tm, d), lambda i: (i, 0))
    return pl.pallas_call(
        functools.partial(_ffn_kernel, tf=tf),
        out_shape=jax.ShapeDtypeStruct((t, d), F32),
        grid=(t // tm,),
        in_specs=[row, _resident((1, d)), _layer_weight((d, 2 * d_ff), layer), _layer_weight((d_ff, d), layer)],
        out_specs=row,
        scratch_shapes=[pltpu.VMEM((tm, d), F32)],
        compiler_params=_params("arbitrary"),
        name="ffn",
    )(x, g.reshape(1, d), w13, w2)


_C_MO, _C_POOL, _C_DV, _C_GATE, _C_END = 0, 256, 512, 1024, 4096
_R_IF, _R_QK, _R_V, _R_DQ, _R_DK, _R_END = 0, 8, 520, 776, 1288, 1800


def _inproj_kernel(x_ref, g_ref, w_ref, wt_ref, cw_ref, cb_ref, gb_ref, cos_ref, sin_ref,
                   mq_ref, mk_ref, mv_ref, mo_ref, pool_ref, mg_ref, dq_ref, dk_ref, dv_ref, gate_ref,
                   ext_ref, *, blocks_per_seq):
    tm = x_ref.shape[0]
    halo = ext_ref.shape[1] - tm
    kconv = cw_ref.shape[0]
    mw = mq_ref.shape[0]
    xn = _rms(x_ref[...], g_ref[...]).astype(BF16)

    def proj(lo, hi):
        return _dot(xn, w_ref[:, lo:hi])

    zt = _dot_nt(wt_ref[...], xn)
    mv_ref[...] = zt[_R_V:_R_DQ, :].astype(BF16)

    lane8 = lax.broadcasted_iota(jnp.int32, (8, MLSTM_CHUNK), 1)
    for c in range(tm // MLSTM_CHUNK):
        sl = slice(c * MLSTM_CHUNK, (c + 1) * MLSTM_CHUNK)
        g8 = zt[_R_IF:_R_QK, sl] + gb_ref[...]
        logf = pltpu.roll(jnp.minimum(g8, 0.0) - jnp.log1p(jnp.exp(-jnp.abs(g8))), N_HEADS, 0)
        bcum = _lane_scan(logf, lane8, jnp.add, 0.0)
        cc = g8 - bcum
        mg_ref[0:8, sl] = bcum
        mg_ref[8:16, sl] = cc
        mg_ref[16:24, sl] = jnp.broadcast_to(jnp.sum(logf, axis=1, keepdims=True), cc.shape)
        mg_ref[24:32, sl] = jnp.broadcast_to(jnp.max(cc, axis=1, keepdims=True), cc.shape)

    @pl.when(pl.program_id(0) % blocks_per_seq == 0)
    def _():
        ext_ref[:, 0:halo] = jnp.zeros((ext_ref.shape[0], halo), F32)

    ext_ref[:, halo:halo + tm] = zt[_R_QK:_R_V, :]
    ext = ext_ref[...]
    taps = [ext if j == kconv - 1 else pltpu.roll(ext, kconv - 1 - j, 1) for j in range(kconv)]
    ext_ref[:, 0:halo] = ext_ref[:, tm:tm + halo]
    for c in range(tm // LANES):
        conv = cb_ref[...]
        for j in range(kconv):
            conv = conv + cw_ref[j] * taps[j][:, halo + c * LANES:halo + (c + 1) * LANES]
        qk = conv * _sigmoid(conv)
        mq_ref[:, c * LANES:(c + 1) * LANES] = qk[0:mw, :].astype(BF16)
        mk_ref[c * LANES:(c + 1) * LANES, :] = jnp.transpose(qk[mw:2 * mw, :] * (HEAD_DIM ** -0.5)).astype(BF16)
    mo_ref[...] = proj(_C_MO, _C_POOL)
    pool_ref[...] = proj(_C_POOL, _C_DV)

    half = HEAD_DIM // 2
    cos = cos_ref[...]
    sin = sin_ref[...]
    for ref, lo, scale in ((dq_ref, _R_DQ, HEAD_DIM ** -0.5 * LOG2E), (dk_ref, _R_DK, 1.0)):
        for h in range(N_HEADS):
            parts = []
            for comp in range(2):
                base = lo + h * LANES + comp * HEAD_DIM
                x1 = zt[base:base + half, :]
                x2 = zt[base + half:base + HEAD_DIM, :]
                parts += [x1 * cos - x2 * sin, x2 * cos + x1 * sin]
            head = jnp.concatenate(parts, axis=0) * scale
            for c in range(tm // LANES):
                ref[c * LANES:(c + 1) * LANES, h * LANES:(h + 1) * LANES] = jnp.transpose(
                    head[:, c * LANES:(c + 1) * LANES]).astype(BF16)
    dv_ref[...] = proj(_C_DV, _C_GATE).astype(BF16)
    gate_ref[...] = proj(_C_GATE, _C_END).astype(BF16)


def _inproj(x, g, w_big, w_t, conv_w, conv_b, gate_b, cosf, sinf, *, seq, tm=512):
    t, d = x.shape
    tm = min(tm, seq)
    mw = (_R_V - _R_QK) // 2
    kconv = conv_w.shape[0]

    def rows(width):
        return pl.BlockSpec((tm, width), lambda i: (i, 0))

    def cols(height):
        return pl.BlockSpec((height, tm), lambda i: (0, i))

    out_shape = [jax.ShapeDtypeStruct((mw, t), BF16), jax.ShapeDtypeStruct((t, mw), BF16),
                 jax.ShapeDtypeStruct((mw, t), BF16)]
    out_specs = [cols(mw), rows(mw), cols(mw)]
    for width in (_C_POOL - _C_MO, _C_DV - _C_POOL):
        out_shape.append(jax.ShapeDtypeStruct((t, width), F32))
        out_specs.append(rows(width))
    out_shape.append(jax.ShapeDtypeStruct((4 * (_R_QK - _R_IF), t), F32))
    out_specs.append(cols(4 * (_R_QK - _R_IF)))
    for w in (512, 512, 512, 3072):
        out_shape.append(jax.ShapeDtypeStruct((t, w), BF16))
        out_specs.append(rows(w))
    return pl.pallas_call(
        functools.partial(_inproj_kernel, blocks_per_seq=seq // tm),
        out_shape=out_shape,
        grid=(t // tm,),
        in_specs=[rows(d), _resident((1, d)), _resident(w_big.shape), _resident(w_t.shape),
                  _resident((kconv, 2 * mw, LANES)), _resident((2 * mw, LANES)), _resident((8, LANES)),
                  cols(HEAD_DIM // 2), cols(HEAD_DIM // 2)],
        out_specs=out_specs,
        scratch_shapes=[pltpu.VMEM((2 * mw, tm + LANES), F32)],
        compiler_params=_params("arbitrary"),
        name="inproj",
    )(x, g.reshape(1, d), w_big, w_t, _on_lanes(conv_w), _on_lanes(conv_b), _on_lanes(gate_b), cosf, sinf)


def _head_of_lane(shape, dim):
    return lax.broadcasted_iota(jnp.int32, shape, dim) // HEAD_DIM


def _mlstm_kernel(q_ref, k_ref, v_ref, o_ref, g_ref, ng_ref, y_ref, c_ref, n_ref, m_ref, *, tb):
    L = MLSTM_CHUNK
    H, D = N_HEADS, HEAD_DIM
    W = H * D

    @pl.when(pl.program_id(1) == 0)
    def _():
        c_ref[...] = jnp.zeros_like(c_ref)
        n_ref[...] = jnp.zeros_like(n_ref)
        m_ref[...] = jnp.zeros_like(m_ref)

    row8 = lax.broadcasted_iota(jnp.int32, (8, L), 0)
    ss = lax.broadcasted_iota(jnp.int32, (L, L), 0)
    tt = lax.broadcasted_iota(jnp.int32, (L, L), 1)
    causal = ss <= tt
    diag = ss == tt
    lane_head = _head_of_lane((L, W), 1)
    bd_mask = _head_of_lane((W, W), 0) == _head_of_lane((W, W), 1)
    n_mask = lax.broadcasted_iota(jnp.int32, (8, W), 0) == _head_of_lane((8, W), 1)
    ones8 = jnp.ones((8, L), BF16)

    def head_rows(x8):
        col = jnp.concatenate([jnp.broadcast_to(x8[h:h + 1, :], (D, L)) for h in range(H)], axis=0)
        return jnp.concatenate([col] * (W // L), axis=1)

    def wide(x8):
        return jnp.concatenate([x8] * (W // L), axis=1)

    cmat = c_ref[...]
    n8 = n_ref[...]
    m8 = m_ref[...]

    for c in range(tb // L):
        q_t = q_ref[:, c * L:(c + 1) * L]
        k_n = k_ref[c * L:(c + 1) * L, :]
        v_t = v_ref[:, c * L:(c + 1) * L]

        bcum = g_ref[0:8, c * L:(c + 1) * L]
        cc = g_ref[8:16, c * L:(c + 1) * L]
        blast = g_ref[16:24, c * L:(c + 1) * L]
        call = g_ref[24:32, c * L:(c + 1) * L]
        gmax = blast + call
        wk = jnp.exp(cc - call)

        k_stack = jnp.concatenate([jnp.where(lane_head == h, k_n, jnp.zeros_like(k_n)) for h in range(H)], axis=0)
        st_all = _dot(k_stack, q_t)
        r_t, den0, cmax = [], jnp.zeros((8, L), F32), jnp.zeros((8, L), F32)
        for h in range(H):
            c_col = jnp.sum(jnp.where(diag, cc[h:h + 1, :], 0.0), axis=1, keepdims=True)
            c_msk = jnp.where(causal, c_col, -jnp.inf)
            c_top = jnp.max(c_msk, axis=0, keepdims=True)
            cmax = jnp.where(row8 == h, c_top, cmax)
            dt = jnp.exp(c_msk - c_top)
            sc = (st_all[h * L:(h + 1) * L, :] * dt).astype(BF16)
            lhs = jnp.concatenate([v_t[h * D:(h + 1) * D, :], ones8], axis=0)
            out = _dot(lhs, sc)
            r_t.append(out[0:D, :])
            den0 = jnp.where(row8 == h, out[D:D + 1, :], den0)
        v_f = v_t.astype(F32)
        vw = jnp.concatenate([v_f[h * D:(h + 1) * D, :] * wk[h:h + 1, :] for h in range(H)] + [wk], axis=0)
        dcn = _dot(vw.astype(BF16), k_n)
        dc = jnp.where(bd_mask, dcn[0:W, :], 0.0)
        dn = jnp.where(n_mask, dcn[W:W + 8, :], 0.0)

        qcn = _dot(jnp.concatenate([cmat, n8], axis=0).astype(BF16), q_t)
        qn = qcn[W:W + 8, :]
        mm = jnp.maximum(m8, cmax)
        a = jnp.exp(cmax - mm)
        b = jnp.exp(m8 - mm)
        den = a * den0 + b * qn
        inv = 1.0 / jnp.maximum(jnp.abs(den), jnp.exp(-bcum - mm))
        ai = a * inv
        bi = b * inv
        m_new = jnp.maximum(blast + m8, gmax)
        decay = jnp.exp(blast + m8 - m_new)
        grow = jnp.exp(gmax - m_new)
        cmat = head_rows(decay) * cmat + head_rows(grow) * dc
        n8 = wide(decay) * n8 + wide(grow) * dn
        m8 = m_new

        y_t = []
        for h in range(H):
            num = ai[h:h + 1, :] * r_t[h] + bi[h:h + 1, :] * qcn[h * D:(h + 1) * D, :]
            dlt = num - jnp.mean(num, axis=0, keepdims=True)
            y_t.append(dlt * lax.rsqrt(jnp.mean(dlt * dlt, axis=0, keepdims=True) + EPS))
        y = jnp.transpose(jnp.concatenate(y_t, axis=0) * ng_ref[...])
        y_ref[c * L:(c + 1) * L, :] = (_sigmoid(o_ref[c * L:(c + 1) * L, :]) * y).astype(y_ref.dtype)

    c_ref[...] = cmat
    n_ref[...] = n8
    m_ref[...] = m8


def _mlstm(mq_t, mk, mv_t, mo, mg_t, norm_g, *, batch, tb=512):
    w, t = mq_t.shape
    s = t // batch
    tb = min(tb, s)
    nb = s // tb

    def cols(rows):
        return pl.BlockSpec((rows, tb), lambda b, i: (0, b * nb + i))

    row_spec = pl.BlockSpec((tb, w), lambda b, i: (b * nb + i, 0))
    return pl.pallas_call(
        functools.partial(_mlstm_kernel, tb=tb),
        out_shape=jax.ShapeDtypeStruct((t, w), BF16),
        grid=(batch, nb),
        in_specs=[cols(w), row_spec, cols(w), row_spec, cols(mg_t.shape[0]), _resident((w, LANES))],
        out_specs=row_spec,
        scratch_shapes=[pltpu.VMEM((w, w), F32), pltpu.VMEM((8, w), F32), pltpu.VMEM((8, LANES), F32)],
        compiler_params=_params("arbitrary", "arbitrary"),
        name="mlstm",
    )(mq_t, mk, mv_t, mo, mg_t, _on_lanes(norm_g))


def _attn_kernel(ti_ref, tj_ref, q_ref, k_ref, v_ref, lam_ref, ng_ref, o_ref, qs_ref, va_ref, m_ref, acc_ref,
                 s0_ref, s1_ref, *, tq, rc, n_off, n_diag, lambda_init):
    nq = q_ref.shape[0] // tq
    comp_a = lax.broadcasted_iota(jnp.int32, (tq, LANES), 1) < HEAD_DIM

    def stack_q(i, carry):
        q = q_ref[pl.ds(pl.multiple_of(i * tq, tq), tq), :]
        zero = jnp.zeros_like(q)
        base = pl.multiple_of(i * (2 * tq), 2 * tq)
        qs_ref[pl.ds(base, tq), :] = jnp.where(comp_a, q, zero)
        qs_ref[pl.ds(base + tq, tq), :] = jnp.where(comp_a, zero, q)
        return carry

    lax.fori_loop(0, nq, stack_q, 0)
    va_ref[:, 0:LANES] = v_ref[...]
    va_ref[:, LANES:] = jnp.ones((va_ref.shape[0], LANES), va_ref.dtype)
    m_ref[...] = jnp.full_like(m_ref, -jnp.inf)
    acc_ref[...] = jnp.zeros_like(acc_ref)

    def scores(buf_ref, t):
        q_row = pl.multiple_of(ti_ref[t] * (2 * tq), 2 * tq)
        k_row = pl.multiple_of(tj_ref[t] * tq, tq)
        buf_ref[...] = _dot_nt(qs_ref[pl.ds(q_row, 2 * tq), :], k_ref[pl.ds(k_row, tq), :])

    def update(buf_ref, t, diagonal):
        q_row = ti_ref[t] * (2 * tq)
        k_row = pl.multiple_of(tj_ref[t] * tq, tq)
        for r in range(2 * tq // rc):
            rows = pl.ds(pl.multiple_of(q_row + r * rc, rc), rc)
            first = (r * rc) % tq
            nk = first + rc if diagonal else tq
            s = buf_ref[r * rc:(r + 1) * rc, 0:nk]
            if diagonal:
                row = first + lax.broadcasted_iota(jnp.int32, s.shape, 0)
                col = lax.broadcasted_iota(jnp.int32, s.shape, 1)
                s = jnp.where(col <= row, s, -jnp.inf)
            m_prev = m_ref[rows, :]
            m_next = jnp.maximum(m_prev, jnp.max(s, axis=-1, keepdims=True))
            alpha = jnp.exp2(m_prev - m_next)
            p = jnp.exp2(s - jnp.concatenate([m_next] * (nk // LANES), axis=1))
            pv = _dot(p.astype(BF16), va_ref[pl.ds(k_row, nk), :])
            acc_ref[rows, :] = jnp.concatenate([alpha, alpha], axis=1) * acc_ref[rows, :] + pv
            m_ref[rows, :] = m_next

    bufs = (s0_ref, s1_ref)

    def run(t0, count, diagonal, cur):
        unroll = max([u for u in range(2, ATTN_UNROLL + 1, 2) if count % u == 0], default=2)

        def steps(t, n, cur):
            for k in range(n):
                scores(bufs[1 - cur], t + k + 1)
                update(bufs[cur], t + k, diagonal)
                cur = 1 - cur
            return cur

        def body(pp, carry):
            steps(t0 + unroll * pp, unroll, cur)
            return carry

        lax.fori_loop(0, count // unroll, body, 0)
        return steps(t0 + count - count % unroll, count % unroll, cur)

    scores(s0_ref, 0)
    cur = run(0, n_off, False, 0)
    run(n_off, n_diag, True, cur)

    lam = lam_ref[...]
    lam_full = (jnp.exp(jnp.sum(lam[0:1, :] * lam[1:2, :], axis=-1, keepdims=True))
                - jnp.exp(jnp.sum(lam[2:3, :] * lam[3:4, :], axis=-1, keepdims=True)) + lambda_init)

    def finish(i, carry):
        base = pl.multiple_of(i * (2 * tq), 2 * tq)
        acc_a = acc_ref[pl.ds(base, tq), :]
        acc_b = acc_ref[pl.ds(base + tq, tq), :]
        out = acc_a[:, 0:LANES] / acc_a[:, LANES:] - lam_full * (acc_b[:, 0:LANES] / acc_b[:, LANES:])
        out = out * lax.rsqrt(jnp.mean(out * out, axis=-1, keepdims=True) + EPS) * ng_ref[...] * (1.0 - lambda_init)
        o_ref[pl.ds(pl.multiple_of(i * tq, tq), tq), :] = out.astype(o_ref.dtype)
        return carry

    lax.fori_loop(0, nq, finish, 0)


def _attn(dq, dk, dv, lam, norm_g, lambda_init, *, batch, tq=512, rc=256):
    t, w = dq.shape
    s = t // batch
    tq = min(tq, s)
    nq = s // tq
    below = [(i, j) for j in range(nq) for i in range(j + 1, nq)]
    pairs = below + [(i, i) for i in range(nq)] + [(0, 0)]
    tab_i = jnp.asarray([p[0] for p in pairs], jnp.int32)
    tab_j = jnp.asarray([p[1] for p in pairs], jnp.int32)
    head = pl.BlockSpec((s, LANES), lambda b, h, ti, tj: (b, h))
    grid_spec = pltpu.PrefetchScalarGridSpec(
        num_scalar_prefetch=2,
        grid=(batch, N_HEADS),
        in_specs=[head, head, head,
                  pl.BlockSpec(lam.shape, lambda b, h, ti, tj: (0, 0)),
                  pl.BlockSpec((1, LANES), lambda b, h, ti, tj: (0, 0))],
        out_specs=head,
        scratch_shapes=[pltpu.VMEM((2 * s, LANES), BF16), pltpu.VMEM((s, 2 * LANES), BF16),
                        pltpu.VMEM((2 * s, LANES), F32), pltpu.VMEM((2 * s, 2 * LANES), F32),
                        pltpu.VMEM((2 * tq, tq), F32), pltpu.VMEM((2 * tq, tq), F32)],
    )
    return pl.pallas_call(
        functools.partial(_attn_kernel, tq=tq, rc=rc, n_off=len(below), n_diag=nq, lambda_init=lambda_init),
        out_shape=jax.ShapeDtypeStruct((t, w), BF16),
        grid_spec=grid_spec,
        compiler_params=_params("arbitrary", "arbitrary"),
        name="diff_attn",
    )(tab_i, tab_j, dq, dk, dv, lam, norm_g.reshape(1, LANES))


def _merge_kernel(x_ref, pu_ref, halo_ref, yb_ref, yc_ref, gate_ref, pbd_ref, ps_ref, pa_ref, pb_ref, pc_ref,
                  wo_ref, g_ref, w13_ref, w2_ref, *rest, tm, seq, tf, final):
    if final:
        fg_ref, o_ref, acc_ref = rest
    else:
        o_ref, acc_ref = rest
    d = x_ref.shape[1]
    t0 = (pl.program_id(0) * tm) % seq
    u = pu_ref[...]
    halo = jnp.where(t0 > 0, halo_ref[...], 0.0)
    e = jnp.concatenate([halo, u], axis=0)
    lane_grp = lax.broadcasted_iota(jnp.int32, u.shape, 1) // (u.shape[1] // len(POOL_WINDOWS))
    win = None
    shift = 1
    for g in range(len(POOL_WINDOWS)):
        e = e + pltpu.roll(e, shift, 0)
        shift *= 2
        cur = e[POOL_HALO:, :]
        win = cur if win is None else jnp.where(lane_grp >= g, cur, win)
    pos1 = t0 + 1 + lax.broadcasted_iota(jnp.int32, u.shape, 0)
    cnt = jnp.minimum(pos1, jnp.left_shift(2, lane_grp)).astype(F32)
    pooled = win / cnt - u
    ya = _dot(pooled.astype(BF16), pbd_ref[...]) * ps_ref[...]

    def gate(j):
        return _sigmoid(gate_ref[:, j * d:(j + 1) * d].astype(F32))

    merged = gate(0) * _dot(ya.astype(BF16), pa_ref[...])
    merged = merged + gate(1) * _dot(yb_ref[...], pb_ref[...])
    merged = merged + gate(2) * _dot(yc_ref[...], pc_ref[...])
    x = x_ref[...] + _dot(merged.astype(BF16), wo_ref[...])
    y = _swiglu_half_step(x, g_ref, w13_ref, w2_ref, acc_ref, tf)
    if final:
        y = _rms(y, fg_ref[...])
    o_ref[...] = y


def _merge_ffn(x, pool_u, yb, yc, gate, pool_bd, pool_scale, p_a, p_b, p_c, w_out, g, w13, w2, layer, final_g=None,
               *, seq, tm=512, tf=256):
    t, d = x.shape
    tm = min(tm, seq)
    cp = pool_u.shape[1]
    hb = tm // POOL_HALO
    d_ff = w2.shape[1]
    final = final_g is not None

    def rows(width):
        return pl.BlockSpec((tm, width), lambda i: (i, 0))

    in_specs = [rows(d), rows(cp), pl.BlockSpec((POOL_HALO, cp), lambda i: (jnp.maximum(i * hb - 1, 0), 0)),
                rows(yb.shape[1]), rows(yc.shape[1]), rows(gate.shape[1]),
                _resident(pool_bd.shape), _resident((1, cp)), _resident(p_a.shape), _resident(p_b.shape),
                _resident(p_c.shape), _resident(w_out.shape),
                _resident((1, d)), _layer_weight((d, 2 * d_ff), layer), _layer_weight((d_ff, d), layer)]
    args = [x, pool_u, pool_u, yb, yc, gate, pool_bd, pool_scale.reshape(1, cp), p_a, p_b, p_c, w_out,
            g.reshape(1, d), w13, w2]
    if final:
        in_specs.append(_resident((1, d)))
        args.append(final_g.reshape(1, d))
    return pl.pallas_call(
        functools.partial(_merge_kernel, tm=tm, seq=seq, tf=tf, final=final),
        out_shape=jax.ShapeDtypeStruct((t, d), F32),
        grid=(t // tm,),
        in_specs=in_specs,
        out_specs=rows(d),
        scratch_shapes=[pltpu.VMEM((tm, d), F32)],
        compiler_params=_params("arbitrary"),
        name="merge_ffn_final" if final else "merge_ffn",
    )(*args)


def _prep_layer(w_in, pool_w):
    d = w_in.shape[0]
    n_pool = pool_w.shape[0] * pool_w.shape[1]
    mw = N_HEADS * HEAD_DIM
    sizes = (n_pool, mw, mw, mw, mw, 2 * N_HEADS, 2 * mw, 2 * mw, 2 * mw, 3 * d)
    offs = np.concatenate([[0], np.cumsum(sizes)])
    u_pool, m_q, m_k, m_v, m_o, m_if, d_q, d_k, d_v, gate = (w_in[:, offs[j]:offs[j + 1]] for j in range(len(sizes)))
    w_big = jnp.concatenate([m_o, u_pool, d_v, gate], axis=1).astype(BF16)
    w_t = jnp.concatenate([m_if, m_q, m_k, m_v, d_q, d_k], axis=1).T.astype(BF16)
    pool_bd = jax.scipy.linalg.block_diag(*[pool_w[g] for g in range(pool_w.shape[0])]).astype(BF16)
    return w_big, w_t, pool_bd


def kernel(x, positions, ffn1_norm, ffn1_w13, ffn1_w2, mix_norm, w_in, pool_w, pool_scale, m_conv_w, m_conv_b,
           m_gate_b, m_norm, d_lambda, d_norm, p_a, p_b, p_c, w_out, ffn2_norm, ffn2_w13, ffn2_w2, final_norm):
    batch, seq, d = x.shape
    depth = w_in.shape[0]
    cosf, sinf = _rope_tables(positions)
    h = x.reshape(batch * seq, d)
    ffn1_w13, ffn1_w2, ffn2_w13, ffn2_w2 = (w.astype(BF16) for w in (ffn1_w13, ffn1_w2, ffn2_w13, ffn2_w2))
    for l in range(depth):
        lambda_init = 0.8 - 0.6 * math.exp(-0.3 * l)
        w_big, w_t, pool_bd = _prep_layer(w_in[l], pool_w[l])
        h = _ffn(h, ffn1_norm[l], ffn1_w13, ffn1_w2, l)
        mq, mk, mv, mo, pool_u, mg, dq, dk, dv, gate = _inproj(h, mix_norm[l], w_big, w_t, m_conv_w[l], m_conv_b[l],
                                                              m_gate_b[l], cosf, sinf, seq=seq)
        yb = _mlstm(mq, mk, mv, mo, mg, m_norm[l], batch=batch)
        yc = _attn(dq, dk, dv, d_lambda[l], d_norm[l], lambda_init, batch=batch)
        h = _merge_ffn(h, pool_u, yb, yc, gate, pool_bd, pool_scale[l], p_a[l].astype(BF16), p_b[l].astype(BF16),
                       p_c[l].astype(BF16), w_out[l].astype(BF16), ffn2_norm[l], ffn2_w13, ffn2_w2, l,
                       final_norm if l == depth - 1 else None, seq=seq)
    return h.reshape(batch, seq, d)
```

```python
import functools
import math

import jax
import jax.numpy as jnp
import numpy as np
from jax import lax
from jax.experimental import pallas as pl
from jax.experimental.pallas import tpu as pltpu

F32 = jnp.float32
BF16 = jnp.bfloat16
EPS = 1e-6
ROPE_THETA = 10000.0
POOL_WINDOWS = (2, 4, 8, 16)
N_HEADS = 4
HEAD_DIM = 64
LANES = 128
MLSTM_CHUNK = 128
ATTN_UNROLL = 14
LOG2E = math.log2(math.e)
POOL_HALO = 16
VMEM_LIMIT = 56 * 1024 * 1024


def _params(*sem):
    return pltpu.CompilerParams(dimension_semantics=sem, vmem_limit_bytes=VMEM_LIMIT)


def _resident(shape):
    zeros = (0,) * len(shape)
    return pl.BlockSpec(shape, lambda *_: zeros, pipeline_mode=pl.Buffered(1))


def _on_lanes(a):
    return jnp.broadcast_to(a[..., None], a.shape + (LANES,))


def _lane_scan(x, lane, combine, fill):
    sh = 1
    while sh < x.shape[1]:
        x = combine(x, jnp.where(lane >= sh, pltpu.roll(x, sh, 1), fill))
        sh *= 2
    return x


def _rms(x, g):
    return x * lax.rsqrt(jnp.mean(x * x, axis=-1, keepdims=True) + EPS) * g


def _sigmoid(x):
    return 1.0 / (1.0 + jnp.exp(-x))


def _dot(a, b):
    return jnp.dot(a, b, preferred_element_type=F32)


def _dot_nt(a, b):
    return lax.dot_general(a, b, (((1,), (1,)), ((), ())), preferred_element_type=F32)


def _rope_kernel(pos_ref, inv_ref, cos_ref, sin_ref):
    tr = pos_ref.shape[1]
    for c in range(tr // LANES):
        sl = slice(c * LANES, (c + 1) * LANES)
        ang = pos_ref[:, sl].astype(F32) * inv_ref[...]
        cos_ref[:, sl] = jnp.cos(ang)
        sin_ref[:, sl] = jnp.sin(ang)


def _rope_tables(positions):
    t = positions.size
    half = HEAD_DIM // 2
    inv = 1.0 / (ROPE_THETA ** (jnp.arange(0, HEAD_DIM, 2, dtype=F32) / HEAD_DIM))
    tr = min(t, 4096)
    spec = pl.BlockSpec((half, tr), lambda i: (0, i))
    return pl.pallas_call(
        _rope_kernel,
        out_shape=[jax.ShapeDtypeStruct((half, t), F32)] * 2,
        grid=(t // tr,),
        in_specs=[pl.BlockSpec((1, tr), lambda i: (0, i)), pl.BlockSpec((half, LANES), lambda i: (0, 0))],
        out_specs=[spec] * 2,
        compiler_params=_params("arbitrary"),
        name="rope_tables",
    )(positions.reshape(1, t), _on_lanes(inv))


def _swiglu_half_step(x, g_ref, w13_ref, w2_ref, acc_ref, tf):
    d_ff = w2_ref.shape[0]
    xn = _rms(x, g_ref[...]).astype(BF16)
    for c in range(d_ff // tf):
        lo, hi = c * tf, (c + 1) * tf
        gate = _dot(xn, w13_ref[:, lo:hi])
        up = _dot(xn, w13_ref[:, d_ff + lo:d_ff + hi])
        act = (gate * _sigmoid(gate) * up).astype(BF16)
        part = _dot(act, w2_ref[lo:hi, :])
        if c == 0:
            acc_ref[...] = part
        else:
            acc_ref[...] += part
    return x + 0.5 * acc_ref[...]


def _ffn_kernel(x_ref, g_ref, w13_ref, w2_ref, o_ref, acc_ref, *, tf):
    o_ref[...] = _swiglu_half_step(x_ref[...], g_ref, w13_ref, w2_ref, acc_ref, tf)


def _layer_weight(shape, layer):
    return pl.BlockSpec((None,) + shape, lambda i: (layer, 0, 0), pipeline_mode=pl.Buffered(1))


def _ffn(x, g, w13, w2, layer, *, tm=512, tf=256):
    t, d = x.shape
    d_ff = w2.shape[1]
    tm = min(tm, t)
    row = pl.BlockSpec((tm, d), lambda i: (i, 0))
    return pl.pallas_call(
        functools.partial(_ffn_kernel, tf=tf),
        out_shape=jax.ShapeDtypeStruct((t, d), F32),
        grid=(t // tm,),
        in_specs=[row, _resident((1, d)), _layer_weight((d, 2 * d_ff), layer), _layer_weight((d_ff, d), layer)],
        out_specs=row,
        scratch_shapes=[pltpu.VMEM((tm, d), F32)],
        compiler_params=_params("arbitrary"),
        name="ffn",
    )(x, g.reshape(1, d), w13, w2)


_C_MO, _C_POOL, _C_DV, _C_GATE, _C_END = 0, 256, 512, 1024, 4096
_R_IF, _R_QK, _R_V, _R_DQ, _R_DK, _R_END = 0, 8, 520, 776, 1288, 1800


def _inproj_kernel(x_ref, g_ref, w_ref, wt_ref, cw_ref, cb_ref, gb_ref, cos_ref, sin_ref,
                   mq_ref, mk_ref, mv_ref, mo_ref, pool_ref, mg_ref, dq_ref, dk_ref, dv_ref, gate_ref,
                   ext_ref, *, blocks_per_seq):
    tm = x_ref.shape[0]
    halo = ext_ref.shape[1] - tm
    kconv = cw_ref.shape[0]
    mw = mq_ref.shape[0]
    xn = _rms(x_ref[...], g_ref[...]).astype(BF16)

    def proj(lo, hi):
        return _dot(xn, w_ref[:, lo:hi])

    zt = _dot_nt(wt_ref[...], xn)
    mv_ref[...] = zt[_R_V:_R_DQ, :].astype(BF16)

    lane8 = lax.broadcasted_iota(jnp.int32, (8, MLSTM_CHUNK), 1)
    for c in range(tm // MLSTM_CHUNK):
        sl = slice(c * MLSTM_CHUNK, (c + 1) * MLSTM_CHUNK)
        g8 = zt[_R_IF:_R_QK, sl] + gb_ref[...]
        logf = pltpu.roll(jnp.minimum(g8, 0.0) - jnp.log1p(jnp.exp(-jnp.abs(g8))), N_HEADS, 0)
        bcum = _lane_scan(logf, lane8, jnp.add, 0.0)
        cc = g8 - bcum
        mg_ref[0:8, sl] = bcum
        mg_ref[8:16, sl] = cc
        mg_ref[16:24, sl] = jnp.broadcast_to(jnp.sum(logf, axis=1, keepdims=True), cc.shape)
        mg_ref[24:32, sl] = jnp.broadcast_to(jnp.max(cc, axis=1, keepdims=True), cc.shape)

    @pl.when(pl.program_id(0) % blocks_per_seq == 0)
    def _():
        ext_ref[:, 0:halo] = jnp.zeros((ext_ref.shape[0], halo), F32)

    ext_ref[:, halo:halo + tm] = zt[_R_QK:_R_V, :]
    ext = ext_ref[...]
    taps = [ext if j == kconv - 1 else pltpu.roll(ext, kconv - 1 - j, 1) for j in range(kconv)]
    ext_ref[:, 0:halo] = ext_ref[:, tm:tm + halo]
    for c in range(tm // LANES):
        conv = cb_ref[...]
        for j in range(kconv):
            conv = conv + cw_ref[j] * taps[j][:, halo + c * LANES:halo + (c + 1) * LANES]
        qk = conv * _sigmoid(conv)
        mq_ref[:, c * LANES:(c + 1) * LANES] = qk[0:mw, :].astype(BF16)
        mk_ref[c * LANES:(c + 1) * LANES, :] = jnp.transpose(qk[mw:2 * mw, :] * (HEAD_DIM ** -0.5)).astype(BF16)
    mo_ref[...] = proj(_C_MO, _C_POOL)
    pool_ref[...] = proj(_C_POOL, _C_DV)

    half = HEAD_DIM // 2
    cos = cos_ref[...]
    sin = sin_ref[...]
    for ref, lo, scale in ((dq_ref, _R_DQ, HEAD_DIM ** -0.5 * LOG2E), (dk_ref, _R_DK, 1.0)):
        for h in range(N_HEADS):
            parts = []
            for comp in range(2):
                base = lo + h * LANES + comp * HEAD_DIM
                x1 = zt[base:base + half, :]
                x2 = zt[base + half:base + HEAD_DIM, :]
                parts += [x1 * cos - x2 * sin, x2 * cos + x1 * sin]
            head = jnp.concatenate(parts, axis=0) * scale
            for c in range(tm // LANES):
                ref[c * LANES:(c + 1) * LANES, h * LANES:(h + 1) * LANES] = jnp.transpose(
                    head[:, c * LANES:(c + 1) * LANES]).astype(BF16)
    dv_ref[...] = proj(_C_DV, _C_GATE).astype(BF16)
    gate_ref[...] = proj(_C_GATE, _C_END).astype(BF16)


def _inproj(x, g, w_big, w_t, conv_w, conv_b, gate_b, cosf, sinf, *, seq, tm=512):
    t, d = x.shape
    tm = min(tm, seq)
    mw = (_R_V - _R_QK) // 2
    kconv = conv_w.shape[0]

    def rows(width):
        return pl.BlockSpec((tm, width), lambda i: (i, 0))

    def cols(height):
        return pl.BlockSpec((height, tm), lambda i: (0, i))

    out_shape = [jax.ShapeDtypeStruct((mw, t), BF16), jax.ShapeDtypeStruct((t, mw), BF16),
                 jax.ShapeDtypeStruct((mw, t), BF16)]
    out_specs = [cols(mw), rows(mw), cols(mw)]
    for width in (_C_POOL - _C_MO, _C_DV - _C_POOL):
        out_shape.append(jax.ShapeDtypeStruct((t, width), F32))
        out_specs.append(rows(width))
    out_shape.append(jax.ShapeDtypeStruct((4 * (_R_QK - _R_IF), t), F32))
    out_specs.append(cols(4 * (_R_QK - _R_IF)))
    for w in (512, 512, 512, 3072):
        out_shape.append(jax.ShapeDtypeStruct((t, w), BF16))
        out_specs.append(rows(w))
    return pl.pallas_call(
        functools.partial(_inproj_kernel, blocks_per_seq=seq // tm),
        out_shape=out_shape,
        grid=(t // tm,),
        in_specs=[rows(d), _resident((1, d)), _resident(w_big.shape), _resident(w_t.shape),
                  _resident((kconv, 2 * mw, LANES)), _resident((2 * mw, LANES)), _resident((8, LANES)),
                  cols(HEAD_DIM // 2), cols(HEAD_DIM // 2)],
        out_specs=out_specs,
        scratch_shapes=[pltpu.VMEM((2 * mw, tm + LANES), F32)],
        compiler_params=_params("arbitrary"),
        name="inproj",
    )(x, g.reshape(1, d), w_big, w_t, _on_lanes(conv_w), _on_lanes(conv_b), _on_lanes(gate_b), cosf, sinf)


def _head_of_lane(shape, dim):
    return lax.broadcasted_iota(jnp.int32, shape, dim) // HEAD_DIM


def _mlstm_kernel(q_ref, k_ref, v_ref, o_ref, g_ref, ng_ref, y_ref, c_ref, n_ref, m_ref, *, tb):
    L = MLSTM_CHUNK
    H, D = N_HEADS, HEAD_DIM
    W = H * D

    @pl.when(pl.program_id(1) == 0)
    def _():
        c_ref[...] = jnp.zeros_like(c_ref)
        n_ref[...] = jnp.zeros_like(n_ref)
        m_ref[...] = jnp.zeros_like(m_ref)

    row8 = lax.broadcasted_iota(jnp.int32, (8, L), 0)
    ss = lax.broadcasted_iota(jnp.int32, (L, L), 0)
    tt = lax.broadcasted_iota(jnp.int32, (L, L), 1)
    causal = ss <= tt
    diag = ss == tt
    lane_head = _head_of_lane((L, W), 1)
    bd_mask = _head_of_lane((W, W), 0) == _head_of_lane((W, W), 1)
    n_mask = lax.broadcasted_iota(jnp.int32, (8, W), 0) == _head_of_lane((8, W), 1)
    ones8 = jnp.ones((8, L), BF16)

    def head_rows(x8):
        col = jnp.concatenate([jnp.broadcast_to(x8[h:h + 1, :], (D, L)) for h in range(H)], axis=0)
        return jnp.concatenate([col] * (W // L), axis=1)

    def wide(x8):
        return jnp.concatenate([x8] * (W // L), axis=1)

    cmat = c_ref[...]
    n8 = n_ref[...]
    m8 = m_ref[...]

    for c in range(tb // L):
        q_t = q_ref[:, c * L:(c + 1) * L]
        k_n = k_ref[c * L:(c + 1) * L, :]
        v_t = v_ref[:, c * L:(c + 1) * L]

        bcum = g_ref[0:8, c * L:(c + 1) * L]
        cc = g_ref[8:16, c * L:(c + 1) * L]
        blast = g_ref[16:24, c * L:(c + 1) * L]
        call = g_ref[24:32, c * L:(c + 1) * L]
        gmax = blast + call
        wk = jnp.exp(cc - call)

        k_stack = jnp.concatenate([jnp.where(lane_head == h, k_n, jnp.zeros_like(k_n)) for h in range(H)], axis=0)
        st_all = _dot(k_stack, q_t)
        r_t, den0, cmax = [], jnp.zeros((8, L), F32), jnp.zeros((8, L), F32)
        for h in range(H):
            c_col = jnp.sum(jnp.where(diag, cc[h:h + 1, :], 0.0), axis=1, keepdims=True)
            c_msk = jnp.where(causal, c_col, -jnp.inf)
            c_top = jnp.max(c_msk, axis=0, keepdims=True)
            cmax = jnp.where(row8 == h, c_top, cmax)
            dt = jnp.exp(c_msk - c_top)
            sc = (st_all[h * L:(h + 1) * L, :] * dt).astype(BF16)
            lhs = jnp.concatenate([v_t[h * D:(h + 1) * D, :], ones8], axis=0)
            out = _dot(lhs, sc)
            r_t.append(out[0:D, :])
            den0 = jnp.where(row8 == h, out[D:D + 1, :], den0)
        v_f = v_t.astype(F32)
        vw = jnp.concatenate([v_f[h * D:(h + 1) * D, :] * wk[h:h + 1, :] for h in range(H)] + [wk], axis=0)
        dcn = _dot(vw.astype(BF16), k_n)
        dc = jnp.where(bd_mask, dcn[0:W, :], 0.0)
        dn = jnp.where(n_mask, dcn[W:W + 8, :], 0.0)

        qcn = _dot(jnp.concatenate([cmat, n8], axis=0).astype(BF16), q_t)
        qn = qcn[W:W + 8, :]
        mm = jnp.maximum(m8, cmax)
        a = jnp.exp(cmax - mm)
        b = jnp.exp(m8 - mm)
        den = a * den0 + b * qn
        inv = 1.0 / jnp.maximum(jnp.abs(den), jnp.exp(-bcum - mm))
        ai = a * inv
        bi = b * inv
        m_new = jnp.maximum(blast + m8, gmax)
        decay = jnp.exp(blast + m8 - m_new)
        grow = jnp.exp(gmax - m_new)
        cmat = head_rows(decay) * cmat + head_rows(grow) * dc
        n8 = wide(decay) * n8 + wide(grow) * dn
        m8 = m_new

        y_t = []
        for h in range(H):
            num = ai[h:h + 1, :] * r_t[h] + bi[h:h + 1, :] * qcn[h * D:(h + 1) * D, :]
            dlt = num - jnp.mean(num, axis=0, keepdims=True)
            y_t.append(dlt * lax.rsqrt(jnp.mean(dlt * dlt, axis=0, keepdims=True) + EPS))
        y = jnp.transpose(jnp.concatenate(y_t, axis=0) * ng_ref[...])
        y_ref[c * L:(c + 1) * L, :] = (_sigmoid(o_ref[c * L:(c + 1) * L, :]) * y).astype(y_ref.dtype)

    c_ref[...] = cmat
    n_ref[...] = n8
    m_ref[...] = m8


def _mlstm(mq_t, mk, mv_t, mo, mg_t, norm_g, *, batch, tb=2048):
    w, t = mq_t.shape
    s = t // batch
    tb = min(tb, s)
    nb = s // tb

    def cols(rows):
        return pl.BlockSpec((rows, tb), lambda b, i: (0, b * nb + i))

    row_spec = pl.BlockSpec((tb, w), lambda b, i: (b * nb + i, 0))
    return pl.pallas_call(
        functools.partial(_mlstm_kernel, tb=tb),
        out_shape=jax.ShapeDtypeStruct((t, w), BF16),
        grid=(batch, nb),
        in_specs=[cols(w), row_spec, cols(w), row_spec, cols(mg_t.shape[0]), _resident((w, LANES))],
        out_specs=row_spec,
        scratch_shapes=[pltpu.VMEM((w, w), F32), pltpu.VMEM((8, w), F32), pltpu.VMEM((8, LANES), F32)],
        compiler_params=_params("arbitrary", "arbitrary"),
        name="mlstm",
    )(mq_t, mk, mv_t, mo, mg_t, _on_lanes(norm_g))


def _attn_kernel(ti_ref, tj_ref, q_ref, k_ref, v_ref, lam_ref, ng_ref, o_ref, qs_ref, va_ref, m_ref, acc_ref,
                 s0_ref, s1_ref, *, tq, rc, n_off, n_diag, lambda_init):
    nq = q_ref.shape[0] // tq
    comp_a = lax.broadcasted_iota(jnp.int32, (tq, LANES), 1) < HEAD_DIM

    def stack_q(i, carry):
        q = q_ref[pl.ds(pl.multiple_of(i * tq, tq), tq), :]
        zero = jnp.zeros_like(q)
        base = pl.multiple_of(i * (2 * tq), 2 * tq)
        qs_ref[pl.ds(base, tq), :] = jnp.where(comp_a, q, zero)
        qs_ref[pl.ds(base + tq, tq), :] = jnp.where(comp_a, zero, q)
        return carry

    lax.fori_loop(0, nq, stack_q, 0)
    va_ref[:, 0:LANES] = v_ref[...]
    va_ref[:, LANES:] = jnp.ones((va_ref.shape[0], LANES), va_ref.dtype)

    def scores(buf_ref, t):
        q_row = pl.multiple_of(ti_ref[t] * (2 * tq), 2 * tq)
        k_row = pl.multiple_of(tj_ref[t] * tq, tq)
        buf_ref[...] = _dot_nt(qs_ref[pl.ds(q_row, 2 * tq), :], k_ref[pl.ds(k_row, tq), :])

    def update(buf_ref, t, diagonal):
        q_row = ti_ref[t] * (2 * tq)
        k_row = pl.multiple_of(tj_ref[t] * tq, tq)
        for r in range(2 * tq // rc):
            rows = pl.ds(pl.multiple_of(q_row + r * rc, rc), rc)
            first = (r * rc) % tq
            nk = first + rc if diagonal else tq
            s = buf_ref[r * rc:(r + 1) * rc, 0:nk]
            if diagonal:
                row = first + lax.broadcasted_iota(jnp.int32, s.shape, 0)
                col = lax.broadcasted_iota(jnp.int32, s.shape, 1)
                s = jnp.where(col <= row, s, -jnp.inf)
                m_next = jnp.broadcast_to(jnp.max(s, axis=-1, keepdims=True), (rc, LANES))
            else:
                m_prev = m_ref[rows, :]
                m_next = jnp.maximum(m_prev, jnp.max(s, axis=-1, keepdims=True))
                alpha = jnp.exp2(m_prev - m_next)
            p = jnp.exp2(s - jnp.concatenate([m_next] * (nk // LANES), axis=1))
            pv = _dot(p.astype(BF16), va_ref[pl.ds(k_row, nk), :])
            if diagonal:
                acc_ref[rows, :] = pv
            else:
                acc_ref[rows, :] = jnp.concatenate([alpha, alpha], axis=1) * acc_ref[rows, :] + pv
            m_ref[rows, :] = m_next

    bufs = (s0_ref, s1_ref)

    def run(t0, count, diagonal, cur):
        unroll = max([u for u in range(2, ATTN_UNROLL + 1, 2) if count % u == 0], default=2)

        def steps(t, n, cur):
            for k in range(n):
                scores(bufs[1 - cur], t + k + 1)
                update(bufs[cur], t + k, diagonal)
                cur = 1 - cur
            return cur

        def body(pp, carry):
            steps(t0 + unroll * pp, unroll, cur)
            return carry

        lax.fori_loop(0, count // unroll, body, 0)
        return steps(t0 + count - count % unroll, count % unroll, cur)

    scores(s0_ref, 0)
    cur = run(0, n_diag, True, 0)
    run(n_diag, n_off, False, cur)

    lam = lam_ref[...]
    lam_full = (jnp.exp(jnp.sum(lam[0:1, :] * lam[1:2, :], axis=-1, keepdims=True))
                - jnp.exp(jnp.sum(lam[2:3, :] * lam[3:4, :], axis=-1, keepdims=True)) + lambda_init)

    def finish(i, carry):
        base = pl.multiple_of(i * (2 * tq), 2 * tq)
        acc_a = acc_ref[pl.ds(base, tq), :]
        acc_b = acc_ref[pl.ds(base + tq, tq), :]
        out = acc_a[:, 0:LANES] / acc_a[:, LANES:] - lam_full * (acc_b[:, 0:LANES] / acc_b[:, LANES:])
        out = out * lax.rsqrt(jnp.mean(out * out, axis=-1, keepdims=True) + EPS) * ng_ref[...] * (1.0 - lambda_init)
        o_ref[pl.ds(pl.multiple_of(i * tq, tq), tq), :] = out.astype(o_ref.dtype)
        return carry

    lax.fori_loop(0, nq, finish, 0)


def _attn(dq, dk, dv, lam, norm_g, lambda_init, *, batch, tq=512, rc=256):
    t, w = dq.shape
    s = t // batch
    tq = min(tq, s)
    nq = s // tq
    below = [(i, j) for j in range(nq) for i in range(j + 1, nq)]
    pairs = [(i, i) for i in range(nq)] + below + [(0, 0)]
    tab_i = jnp.asarray([p[0] for p in pairs], jnp.int32)
    tab_j = jnp.asarray([p[1] for p in pairs], jnp.int32)
    head = pl.BlockSpec((s, LANES), lambda b, h, ti, tj: (b, h))
    grid_spec = pltpu.PrefetchScalarGridSpec(
        num_scalar_prefetch=2,
        grid=(batch, N_HEADS),
        in_specs=[head, head, head,
                  pl.BlockSpec(lam.shape, lambda b, h, ti, tj: (0, 0)),
                  pl.BlockSpec((1, LANES), lambda b, h, ti, tj: (0, 0))],
        out_specs=head,
        scratch_shapes=[pltpu.VMEM((2 * s, LANES), BF16), pltpu.VMEM((s, 2 * LANES), BF16),
                        pltpu.VMEM((2 * s, LANES), F32), pltpu.VMEM((2 * s, 2 * LANES), F32),
                        pltpu.VMEM((2 * tq, tq), F32), pltpu.VMEM((2 * tq, tq), F32)],
    )
    return pl.pallas_call(
        functools.partial(_attn_kernel, tq=tq, rc=rc, n_off=len(below), n_diag=nq, lambda_init=lambda_init),
        out_shape=jax.ShapeDtypeStruct((t, w), BF16),
        grid_spec=grid_spec,
        compiler_params=_params("arbitrary", "arbitrary"),
        name="diff_attn",
    )(tab_i, tab_j, dq, dk, dv, lam, norm_g.reshape(1, LANES))


def _merge_kernel(x_ref, pu_ref, halo_ref, yb_ref, yc_ref, gate_ref, pbd_ref, ps_ref, pa_ref, pb_ref, pc_ref,
                  wo_ref, g_ref, w13_ref, w2_ref, *rest, tm, seq, tf, final):
    if final:
        fg_ref, o_ref, acc_ref = rest
    else:
        o_ref, acc_ref = rest
    d = x_ref.shape[1]
    t0 = (pl.program_id(0) * tm) % seq
    u = pu_ref[...]
    halo = jnp.where(t0 > 0, halo_ref[...], 0.0)
    e = jnp.concatenate([halo, u], axis=0)
    lane_grp = lax.broadcasted_iota(jnp.int32, u.shape, 1) // (u.shape[1] // len(POOL_WINDOWS))
    win = None
    shift = 1
    for g in range(len(POOL_WINDOWS)):
        e = e + pltpu.roll(e, shift, 0)
        shift *= 2
        cur = e[POOL_HALO:, :]
        win = cur if win is None else jnp.where(lane_grp >= g, cur, win)
    pos1 = t0 + 1 + lax.broadcasted_iota(jnp.int32, u.shape, 0)
    cnt = jnp.minimum(pos1, jnp.left_shift(2, lane_grp)).astype(F32)
    pooled = win / cnt - u
    ya = _dot(pooled.astype(BF16), pbd_ref[...]) * ps_ref[...]

    def gate(j):
        return _sigmoid(gate_ref[:, j * d:(j + 1) * d].astype(F32))

    merged = gate(0) * _dot(ya.astype(BF16), pa_ref[...])
    merged = merged + gate(1) * _dot(yb_ref[...], pb_ref[...])
    merged = merged + gate(2) * _dot(yc_ref[...], pc_ref[...])
    x = x_ref[...] + _dot(merged.astype(BF16), wo_ref[...])
    y = _swiglu_half_step(x, g_ref, w13_ref, w2_ref, acc_ref, tf)
    if final:
        y = _rms(y, fg_ref[...])
    o_ref[...] = y


def _merge_ffn(x, pool_u, yb, yc, gate, pool_bd, pool_scale, p_a, p_b, p_c, w_out, g, w13, w2, layer, final_g=None,
               *, seq, tm=512, tf=256):
    t, d = x.shape
    tm = min(tm, seq)
    cp = pool_u.shape[1]
    hb = tm // POOL_HALO
    d_ff = w2.shape[1]
    final = final_g is not None

    def rows(width):
        return pl.BlockSpec((tm, width), lambda i: (i, 0))

    in_specs = [rows(d), rows(cp), pl.BlockSpec((POOL_HALO, cp), lambda i: (jnp.maximum(i * hb - 1, 0), 0)),
                rows(yb.shape[1]), rows(yc.shape[1]), rows(gate.shape[1]),
                _resident(pool_bd.shape), _resident((1, cp)), _resident(p_a.shape), _resident(p_b.shape),
                _resident(p_c.shape), _resident(w_out.shape),
                _resident((1, d)), _layer_weight((d, 2 * d_ff), layer), _layer_weight((d_ff, d), layer)]
    args = [x, pool_u, pool_u, yb, yc, gate, pool_bd, pool_scale.reshape(1, cp), p_a, p_b, p_c, w_out,
            g.reshape(1, d), w13, w2]
    if final:
        in_specs.append(_resident((1, d)))
        args.append(final_g.reshape(1, d))
    return pl.pallas_call(
        functools.partial(_merge_kernel, tm=tm, seq=seq, tf=tf, final=final),
        out_shape=jax.ShapeDtypeStruct((t, d), F32),
        grid=(t // tm,),
        in_specs=in_specs,
        out_specs=rows(d),
        scratch_shapes=[pltpu.VMEM((tm, d), F32)],
        compiler_params=_params("arbitrary"),
        name="merge_ffn_final" if final else "merge_ffn",
    )(*args)


def _prep_layer(w_in, pool_w):
    d = w_in.shape[0]
    n_pool = pool_w.shape[0] * pool_w.shape[1]
    mw = N_HEADS * HEAD_DIM
    sizes = (n_pool, mw, mw, mw, mw, 2 * N_HEADS, 2 * mw, 2 * mw, 2 * mw, 3 * d)
    offs = np.concatenate([[0], np.cumsum(sizes)])
    u_pool, m_q, m_k, m_v, m_o, m_if, d_q, d_k, d_v, gate = (w_in[:, offs[j]:offs[j + 1]] for j in range(len(sizes)))
    w_big = jnp.concatenate([m_o, u_pool, d_v, gate], axis=1).astype(BF16)
    w_t = jnp.concatenate([m_if, m_q, m_k, m_v, d_q, d_k], axis=1).T.astype(BF16)
    pool_bd = jax.scipy.linalg.block_diag(*[pool_w[g] for g in range(pool_w.shape[0])]).astype(BF16)
    return w_big, w_t, pool_bd


def kernel(x, positions, ffn1_norm, ffn1_w13, ffn1_w2, mix_norm, w_in, pool_w, pool_scale, m_conv_w, m_conv_b,
           m_gate_b, m_norm, d_lambda, d_norm, p_a, p_b, p_c, w_out, ffn2_norm, ffn2_w13, ffn2_w2, final_norm):
    batch, seq, d = x.shape
    depth = w_in.shape[0]
    cosf, sinf = _rope_tables(positions)
    h = x.reshape(batch * seq, d)
    ffn1_w13, ffn1_w2, ffn2_w13, ffn2_w2 = (w.astype(BF16) for w in (ffn1_w13, ffn1_w2, ffn2_w13, ffn2_w2))
    for l in range(depth):
        lambda_init = 0.8 - 0.6 * math.exp(-0.3 * l)
        w_big, w_t, pool_bd = _prep_layer(w_in[l], pool_w[l])
        h = _ffn(h, ffn1_norm[l], ffn1_w13, ffn1_w2, l)
        mq, mk, mv, mo, pool_u, mg, dq, dk, dv, gate = _inproj(h, mix_norm[l], w_big, w_t, m_conv_w[l], m_conv_b[l],
                                                              m_gate_b[l], cosf, sinf, seq=seq)
        yb = _mlstm(mq, mk, mv, mo, mg, m_norm[l], batch=batch)
        yc = _attn(dq, dk, dv, d_lambda[l], d_norm[l], lambda_init, batch=batch)
        h = _merge_ffn(h, pool_u, yb, yc, gate, pool_bd, pool_scale[l], p_a[l].astype(BF16), p_b[l].astype(BF16),
                       p_c[l].astype(BF16), w_out[l].astype(BF16), ffn2_norm[l], ffn2_w13, ffn2_w2, l,
                       final_norm if l == depth - 1 else None, seq=seq)
    return h.reshape(batch, seq, d)
```
